```python
import jax, jax.numpy as jnp
from jax import lax
import numpy as np

D_MODEL = 1024
BATCH = 8
SEQ = 4096
DEPTH = 4

CHUNK = 64
N_MIXERS = 3
N_HEADS = 16
HEAD_DIM = D_MODEL // N_HEADS
LEFT_CHUNKS = 8
BAND = (LEFT_CHUNKS + 1) * CHUNK
MAX_REL = 256
N_REL = 2 * MAX_REL + 1
IDX_HEADS = 8
IDX_DIM = 64
TOPK_MAX = 256
B_QBLOCK = 32
B_SPLITS = (D_MODEL, 2 * D_MODEL, 3 * D_MODEL,
            3 * D_MODEL + IDX_HEADS * IDX_DIM,
            3 * D_MODEL + IDX_HEADS * IDX_DIM + IDX_DIM)
B_PROJ = 3 * D_MODEL + IDX_HEADS * IDX_DIM + IDX_DIM + IDX_HEADS
C_QBLOCK = 128
D_FF = 2816
CONV_W = 3
ROPE_THETA = 10000.0
EPS = 1e-6
N_A = (DEPTH + 2) // 3
N_B = (DEPTH + 1) // 3
N_C = DEPTH // 3

kernel_name = "hybrid_chunk_causal_interleaved_trunk"


def rmsnorm(x, g):
    xf = x.astype(jnp.float32)
    y = xf * lax.rsqrt(jnp.mean(xf * xf, axis=-1, keepdims=True) + EPS)
    return (y * g.astype(jnp.float32)).astype(x.dtype)


def rope_tables(seq, dim):
    inv = ROPE_THETA ** (-jnp.arange(0, dim, 2, dtype=jnp.float32) / dim)
    ang = jnp.arange(seq, dtype=jnp.float32)[:, None] * inv[None, :]
    return jnp.cos(ang)[:, None, :], jnp.sin(ang)[:, None, :]


def rope(x, cos, sin):
    half = x.shape[-1] // 2
    c = cos.astype(x.dtype)
    s = sin.astype(x.dtype)
    x1, x2 = x[..., :half], x[..., half:]
    return jnp.concatenate([x1 * c - x2 * s, x2 * c + x1 * s], axis=-1)


def mixer_chunk_relbias(h, w_qkv, q_norm, k_norm, rel_bias, w_o):
    bsz, seq, _ = h.shape
    n_chunks = seq // CHUNK
    pad = LEFT_CHUNKS * CHUNK
    q, k, v = jnp.split((h @ w_qkv).reshape(bsz, seq, 3, N_HEADS, HEAD_DIM), 3, axis=2)
    q = rmsnorm(q[:, :, 0], q_norm)
    k = rmsnorm(k[:, :, 0], k_norm)
    v = v[:, :, 0]
    kp = jnp.pad(k, ((0, 0), (pad, 0), (0, 0), (0, 0)))
    vp = jnp.pad(v, ((0, 0), (pad, 0), (0, 0), (0, 0)))
    rel = jnp.clip(jnp.arange(CHUNK)[:, None] - jnp.arange(BAND)[None, :] + pad,
                   -MAX_REL, MAX_REL) + MAX_REL
    bias = rel_bias.astype(jnp.float32)[:, rel]
    scale = HEAD_DIM ** -0.5
    qc = q.reshape(bsz, n_chunks, CHUNK, N_HEADS, HEAD_DIM).swapaxes(0, 1)

    def one_chunk(args):
        c, q_blk = args
        k_band = lax.dynamic_slice_in_dim(kp, c * CHUNK, BAND, axis=1)
        v_band = lax.dynamic_slice_in_dim(vp, c * CHUNK, BAND, axis=1)
        s = jnp.einsum('bqhd,bkhd->bhqk', q_blk, k_band).astype(jnp.float32) * scale + bias
        valid = (c * CHUNK - pad + jnp.arange(BAND)) >= 0
        s = jnp.where(valid[None, None, None, :], s, -jnp.inf)
        p = jax.nn.softmax(s, axis=-1).astype(v_band.dtype)
        return jnp.einsum('bhqk,bkhd->bqhd', p, v_band)

    o = lax.map(one_chunk, (jnp.arange(n_chunks), qc))
    o = o.swapaxes(0, 1).reshape(bsz, seq, D_MODEL)
    return o @ w_o


def mixer_indexed_sparse(h, w_in, q_norm, k_norm, w_o, cos, sin):
    bsz, seq, _ = h.shape
    q, k, v, qi, ki, wi = jnp.split(h @ w_in, B_SPLITS, axis=-1)
    q = rope(rmsnorm(q.reshape(bsz, seq, N_HEADS, HEAD_DIM), q_norm), cos, sin)
    k = rope(rmsnorm(k.reshape(bsz, seq, N_HEADS, HEAD_DIM), k_norm), cos, sin)
    v = v.reshape(bsz, seq, N_HEADS, HEAD_DIM)
    qi = rope(qi.reshape(bsz, seq, IDX_HEADS, IDX_DIM), cos, sin)
    ki = rope(ki.reshape(bsz, seq, 1, IDX_DIM), cos, sin)[:, :, 0]
    wi = wi.astype(jnp.float32) * IDX_HEADS ** -0.5
    topk = min(TOPK_MAX, seq // 4)
    n_blk = seq // B_QBLOCK
    key_pos = jnp.arange(seq)
    scale = HEAD_DIM ** -0.5

    def to_blocks(a):
        return a.reshape(bsz, n_blk, B_QBLOCK, *a.shape[2:]).swapaxes(0, 1)

    def one_block(args):
        blk, q_b, qi_b, wi_b = args
        t = blk * B_QBLOCK + jnp.arange(B_QBLOCK)
        limit = (t // CHUNK + 1) * CHUNK
        logits = jnp.einsum('bqhd,bsd->bqhs', qi_b, ki).astype(jnp.float32) * IDX_DIM ** -0.5
        score = jnp.einsum('bqh,bqhs->bqs', wi_b, jax.nn.relu(logits))
        adm = key_pos[None, :] < limit[:, None]
        score = jnp.where(adm[None], score, -jnp.inf)
        _, idx = lax.top_k(score, topk)
        sel_ok = idx < limit[None, :, None]
        k_sel = jax.vmap(lambda kb, ib: kb[ib])(k, idx)
        v_sel = jax.vmap(lambda vb, ib: vb[ib])(v, idx)
        s = jnp.einsum('bqhd,bqkhd->bhqk', q_b, k_sel).astype(jnp.float32) * scale
        s = jnp.where(sel_ok[:, None], s, -jnp.inf)
        p = jax.nn.softmax(s, axis=-1).astype(v_sel.dtype)
        return jnp.einsum('bhqk,bqkhd->bqhd', p, v_sel)

    o = lax.map(one_block, (jnp.arange(n_blk), to_blocks(q), to_blocks(qi), to_blocks(wi)))
    o = o.swapaxes(0, 1).reshape(bsz, seq, D_MODEL)
    return o @ w_o


def mixer_stick_breaking(h, w_qkv, w_o):
    bsz, seq, _ = h.shape
    q, k, v = jnp.split((h @ w_qkv).reshape(bsz, seq, 3, N_HEADS, HEAD_DIM), 3, axis=2)
    q, k, v = q[:, :, 0], k[:, :, 0], v[:, :, 0]
    n_blk = seq // C_QBLOCK
    key_pos = jnp.arange(seq)
    scale = HEAD_DIM ** -0.5
    qb = q.reshape(bsz, n_blk, C_QBLOCK, N_HEADS, HEAD_DIM).swapaxes(0, 1)

    def one_block(args):
        blk, q_b = args
        t = blk * C_QBLOCK + jnp.arange(C_QBLOCK)
        z = jnp.einsum('bqhd,bshd->bhqs', q_b, k).astype(jnp.float32) * scale
        causal = (key_pos[None, :] < t[:, None])[None, None]
        log_beta = jax.nn.log_sigmoid(z)
        log_keep = jnp.where(causal, jax.nn.log_sigmoid(-z), 0.0)
        rev = lax.cumsum(log_keep, axis=3, reverse=True)
        after = jnp.concatenate([rev[..., 1:], jnp.zeros_like(rev[..., :1])], axis=-1)
        a = jnp.where(causal, jnp.exp(log_beta + after), 0.0)
        return jnp.einsum('bhqs,bshd->bqhd', a.astype(v.dtype), v)

    o = lax.map(one_block, (jnp.arange(n_blk), qb))
    o = o.swapaxes(0, 1).reshape(bsz, seq, D_MODEL)
    return o @ w_o


def conv_ffn(h, w_in, conv_w, conv_b, w_down):
    seq = h.shape[1]
    a = h @ w_in
    ap = jnp.pad(a, ((0, 0), (CONV_W - 1, 0), (0, 0)))
    c = conv_b + sum(ap[:, i:i + seq] * conv_w[i] for i in range(CONV_W))
    g, u = jnp.split(c, 2, axis=-1)
    return (jax.nn.silu(g) * u) @ w_down


def setup_inputs(seed: int = 0) -> dict:
    key = jax.random.key(seed)
    ks = jax.random.split(key, 20)
    f32 = jnp.float32
    D = D_MODEL

    def w(k, shape, fan_in):
        return jax.random.normal(k, shape, f32) * fan_in ** -0.5

    def gain(k, shape):
        return 1.0 + 0.1 * jax.random.normal(k, shape, f32)

    return {
        "x": jax.random.normal(ks[0], (BATCH, SEQ, D), f32),
        "norm1_g": gain(ks[1], (DEPTH, D)),
        "norm2_g": gain(ks[2], (DEPTH, D)),
        "a_w_qkv": w(ks[3], (N_A, D, 3 * D), D),
        "a_q_norm": gain(ks[4], (N_A, HEAD_DIM)),
        "a_k_norm": gain(ks[5], (N_A, HEAD_DIM)),
        "a_rel_bias": 0.1 * jax.random.normal(ks[6], (N_A, N_HEADS, N_REL), f32),
        "a_w_o": w(ks[7], (N_A, D, D), D),
        "b_w_in": w(ks[8], (N_B, D, B_PROJ), D),
        "b_q_norm": gain(ks[9], (N_B, HEAD_DIM)),
        "b_k_norm": gain(ks[10], (N_B, HEAD_DIM)),
        "b_w_o": w(ks[11], (N_B, D, D), D),
        "c_w_qkv": w(ks[12], (N_C, D, 3 * D), D),
        "c_w_o": w(ks[13], (N_C, D, D), D),
        "ffn_w_in": w(ks[14], (DEPTH, D, 2 * D_FF), D),
        "ffn_conv_w": w(ks[15], (DEPTH, CONV_W, 2 * D_FF), CONV_W),
        "ffn_conv_b": 0.01 * jax.random.normal(ks[16], (DEPTH, 2 * D_FF), f32),
        "ffn_w_down": w(ks[17], (DEPTH, D_FF, D), D_FF),
    }


def reference(x, norm1_g, norm2_g, a_w_qkv, a_q_norm, a_k_norm, a_rel_bias, a_w_o,
              b_w_in, b_q_norm, b_k_norm, b_w_o, c_w_qkv, c_w_o,
              ffn_w_in, ffn_conv_w, ffn_conv_b, ffn_w_down):
    seq = x.shape[1]
    cos, sin = rope_tables(seq, HEAD_DIM)
    ia = ib = ic = 0
    for layer in range(DEPTH):
        h = rmsnorm(x, norm1_g[layer])
        kind = layer % N_MIXERS
        if kind == 0:
            y = mixer_chunk_relbias(h, a_w_qkv[ia], a_q_norm[ia], a_k_norm[ia], a_rel_bias[ia], a_w_o[ia])
            ia += 1
        elif kind == 1:
            y = mixer_indexed_sparse(h, b_w_in[ib], b_q_norm[ib], b_k_norm[ib], b_w_o[ib], cos, sin)
            ib += 1
        else:
            y = mixer_stick_breaking(h, c_w_qkv[ic], c_w_o[ic])
            ic += 1
        x = x + y
        h = rmsnorm(x, norm2_g[layer])
        x = x + conv_ffn(h, ffn_w_in[layer], ffn_conv_w[layer], ffn_conv_b[layer], ffn_w_down[layer])
    return x
```

```python
import functools

import jax
import jax.numpy as jnp
from jax import lax
from jax.experimental import pallas as pl
from jax.experimental.pallas import tpu as pltpu

D_MODEL = 1024
N_HEADS = 16
HEAD_DIM = 64
N_PAIRS = N_HEADS // 2
LANES = 128
CHUNK = 64
LEFT_CHUNKS = 8
PAD = LEFT_CHUNKS * CHUNK
MAX_REL = 256
IDX_HEADS = 8
IDX_DIM = 64
TOPK_MAX = 256
D_FF = 2816
CONV_W = 3
ROPE_THETA = 10000.0
EPS = 1e-6
NEG = -1e30

TM = 512
A_TQ = 128
A_W = A_TQ + PAD
B_TQ = 128
B_TK = 128
C_TQ = 256
C_TK = 256
F_TILE = 256
VMEM_LIMIT = 56 * 1024 * 1024

BF16 = jnp.bfloat16
F32 = jnp.float32


def _cparams(sem):
    return pltpu.CompilerParams(dimension_semantics=sem, vmem_limit_bytes=VMEM_LIMIT)


def _const_spec(shape):
    nd = len(shape)
    return pl.BlockSpec(shape, lambda *_: (0,) * nd, pipeline_mode=pl.Buffered(1))


def _dot(a, b):
    return jnp.dot(a, b, preferred_element_type=F32)


def _dot_nt(a, b):
    return lax.dot_general(a, b, (((1,), (1,)), ((), ())), preferred_element_type=F32)


def _rms(x, g):
    return x * lax.rsqrt(jnp.mean(x * x, axis=-1, keepdims=True) + EPS) * g


def _lane_half_masks(dtype):
    lane = lax.broadcasted_iota(jnp.int32, (1, LANES), 1)
    lo = (lane < HEAD_DIM).astype(dtype)
    return lo, (1 - lo).astype(dtype)


def _head_norm(y, bd, gain):
    ss = _dot((y * y).astype(BF16), bd)
    return y * lax.rsqrt(ss * (1.0 / HEAD_DIM) + EPS) * gain


def _rope(y, cos_t, sin_t):
    lane = lax.broadcasted_iota(jnp.int32, y.shape, 1)
    first = (lane % HEAD_DIM) < (HEAD_DIM // 2)
    partner = jnp.where(first, pltpu.roll(y, LANES - HEAD_DIM // 2, 1), pltpu.roll(y, HEAD_DIM // 2, 1))
    return y * cos_t + partner * sin_t


def _proj_body(*refs, modes, use_rope, n_aux, lead):
    it = iter(refs)
    x_ref, g_ref, w_ref, bd_ref, gain_ref = next(it), next(it), next(it), next(it), next(it)
    cos_ref = sin_ref = None
    if use_rope:
        cos_ref, sin_ref = next(it), next(it)
    o_ref = next(it)
    aux_ref = next(it) if n_aux else None
    n_out = 128 * len(modes)

    def compute():
        hb = _rms(x_ref[...], g_ref[...]).astype(BF16)
        bd = bd_ref[...]
        for c0 in range(0, n_out, 512):
            wd = min(512, n_out - c0)
            y = _dot(hb, w_ref[:, c0:c0 + wd])
            for t0 in range(0, wd, LANES):
                norm_idx, rope, scale = modes[(c0 + t0) // LANES]
                yt = y[:, t0:t0 + LANES]
                if norm_idx is not None:
                    yt = _head_norm(yt, bd, gain_ref[norm_idx:norm_idx + 1, :])
                if rope == "all":
                    yt = _rope(yt, cos_ref[...], sin_ref[...])
                elif rope == "low_head":
                    lane = lax.broadcasted_iota(jnp.int32, yt.shape, 1)
                    yt = jnp.where(lane < HEAD_DIM, _rope(yt, cos_ref[...], sin_ref[...]), yt)
                if scale != 1.0:
                    yt = yt * scale
                o_ref[:, c0 + t0:c0 + t0 + LANES] = yt.astype(o_ref.dtype)
                if n_aux and (c0 + t0) // LANES == len(modes) - 1:
                    aux_ref[...] = yt

    if lead:
        @pl.when(pl.program_id(1) < lead)
        def _():
            o_ref[...] = jnp.zeros_like(o_ref)

        pl.when(pl.program_id(1) >= lead)(compute)
    else:
        compute()


def _project(x, g, w, modes, gains, bd, rope_tabs=None, aux=False, lead_rows=0):
    bsz, seq, _ = x.shape
    n_out = LANES * len(modes)
    lead = lead_rows // TM
    nblk = seq // TM
    in_specs = [
        pl.BlockSpec((None, TM, D_MODEL), lambda b, i: (b, jnp.maximum(i - lead, 0), 0)),
        _const_spec((1, D_MODEL)),
        _const_spec((D_MODEL, n_out)),
        _const_spec((LANES, LANES)),
        _const_spec(gains.shape),
    ]
    args = [x, g.reshape(1, D_MODEL), w, bd, gains]
    if rope_tabs is not None:
        in_specs += [pl.BlockSpec((TM, LANES), lambda b, i: (i, 0))] * 2
        args += list(rope_tabs)
    out_shape = [jax.ShapeDtypeStruct((bsz, lead_rows + seq, n_out), BF16)]
    out_specs = [pl.BlockSpec((None, TM, n_out), lambda b, i: (b, i, 0))]
    if aux:
        out_shape.append(jax.ShapeDtypeStruct((bsz, seq, LANES), F32))
        out_specs.append(pl.BlockSpec((None, TM, LANES), lambda b, i: (b, i, 0)))
    res = pl.pallas_call(
        functools.partial(_proj_body, modes=tuple(modes), use_rope=rope_tabs is not None,
                          n_aux=int(aux), lead=lead),
        grid=(bsz, lead + nblk),
        in_specs=in_specs,
        out_specs=out_specs,
        out_shape=out_shape,
        compiler_params=_cparams(("parallel", "arbitrary")),
        name="norm_proj",
    )(*args)
    return res if aux else res[0]


def _attn_a_body(q_ref, k_ref, v_ref, bias_ref, bd_ref, gain_ref, o_ref, kn_ref):
    i = pl.program_id(2)
    bd = bd_ref[...]
    n_rows = kn_ref.shape[0]

    @pl.when(i == 0)
    def _():
        def norm_rows(c, carry):
            r0 = pl.multiple_of(c * TM, TM)
            kk = k_ref[pl.ds(r0, TM), :].astype(F32)
            kn_ref[pl.ds(r0, TM), :] = _head_norm(kk, bd, gain_ref[1:2, :]).astype(BF16)
            return carry
        lax.fori_loop(0, n_rows // TM, norm_rows, 0)

    lo, hi = _lane_half_masks(F32)
    q = _head_norm(q_ref[...].astype(F32), bd, gain_ref[0:1, :]) * (HEAD_DIM ** -0.5)
    w0 = pl.multiple_of(i * A_TQ, A_TQ)
    kw = kn_ref[pl.ds(w0, A_W), :]
    vw = v_ref[pl.ds(w0, A_W), :]
    col = lax.broadcasted_iota(jnp.int32, (1, A_W), 1)
    valid = (col + (i * A_TQ - PAD)) >= 0
    out = jnp.zeros((A_TQ, LANES), F32)
    for half, m in enumerate((lo, hi)):
        s = _dot_nt((q * m).astype(BF16), kw) + bias_ref[half]
        s = jnp.where(valid, s, NEG)
        p = jnp.exp(s - jnp.max(s, axis=-1, keepdims=True))
        l = jnp.sum(p, axis=-1, keepdims=True)
        o = _dot(p.astype(BF16), vw * m.astype(BF16))
        out = out + o * (1.0 / l)
    o_ref[...] = out.astype(o_ref.dtype)


def _attn_a(qkv, bias, bd, gains, bsz, seq):
    rows = PAD + seq
    return pl.pallas_call(
        _attn_a_body,
        grid=(bsz, N_PAIRS, seq // A_TQ),
        in_specs=[
            pl.BlockSpec((None, A_TQ, LANES), lambda b, j, i: (b, i + PAD // A_TQ, j)),
            pl.BlockSpec((None, rows, LANES), lambda b, j, i: (b, 0, N_PAIRS + j)),
            pl.BlockSpec((None, rows, LANES), lambda b, j, i: (b, 0, 2 * N_PAIRS + j)),
            pl.BlockSpec((2, A_TQ, A_W), lambda b, j, i: (j, 0, 0)),
            _const_spec((LANES, LANES)),
            _const_spec(gains.shape),
        ],
        out_specs=pl.BlockSpec((None, A_TQ, LANES), lambda b, j, i: (b, i, j)),
        out_shape=jax.ShapeDtypeStruct((bsz, seq, D_MODEL), BF16),
        scratch_shapes=[pltpu.VMEM((rows, LANES), BF16)],
        compiler_params=_cparams(("parallel", "parallel", "arbitrary")),
        name="attn_band",
    )(qkv, qkv, qkv, bias, bd, gains)


def _band_bias(rel_bias):
    r = jnp.arange(A_TQ)[:, None]
    c = jnp.arange(A_W)[None, :]
    rel = jnp.clip(r - c + PAD, -MAX_REL, MAX_REL) + MAX_REL
    in_band = (c // CHUNK >= r // CHUNK) & (c // CHUNK <= r // CHUNK + LEFT_CHUNKS)
    return jnp.where(in_band[None], rel_bias.astype(F32)[:, rel], NEG)


def _softplus(z):
    return jnp.maximum(z, 0.0) + jnp.log1p(jnp.exp(-jnp.abs(z)))


def _attn_c_body(q_ref, k_ref, v_ref, tri_ref, o_ref):
    i = pl.program_id(2)
    tri = tri_ref[...]
    masks = _lane_half_masks(BF16)
    q = q_ref[...]
    row = lax.broadcasted_iota(jnp.int32, (C_TQ, C_TK), 0)
    colk = lax.broadcasted_iota(jnp.int32, (C_TQ, C_TK), 1)
    causal = colk < row

    def block(kb, qh, m, run, acc, diag):
        r0 = pl.multiple_of(kb * C_TK, C_TK)
        z = _dot_nt(qh, k_ref[pl.ds(r0, C_TK), :])
        sp = _softplus(z)
        if diag:
            sp = jnp.where(causal, sp, 0.0)
        sp_hi = sp.astype(BF16)
        sp_lo = (sp - sp_hi.astype(F32)).astype(BF16)
        tail = _dot(sp_hi, tri) + _dot(sp_lo, tri)
        a = jnp.exp(z - (tail + run))
        if diag:
            a = jnp.where(causal, a, 0.0)
        acc = acc + _dot(a.astype(BF16), v_ref[pl.ds(r0, C_TK), :] * m)
        run = run + jnp.sum(sp, axis=-1, keepdims=True)
        return run, acc

    out = jnp.zeros((C_TQ, LANES), F32)
    for m in masks:
        qh = q * m
        run, acc = block(i, qh, m, jnp.zeros((C_TQ, 1), F32), jnp.zeros((C_TQ, LANES), F32), True)

        def step(n, carry, qh=qh, m=m):
            return block(i - 1 - n, qh, m, carry[0], carry[1], False)

        run, acc = lax.fori_loop(0, i, step, (run, acc))
        out = out + acc
    o_ref[...] = out.astype(o_ref.dtype)


def _attn_c(qkv, tri, bsz, seq):
    return pl.pallas_call(
        _attn_c_body,
        grid=(bsz, N_PAIRS, seq // C_TQ),
        in_specs=[
            pl.BlockSpec((None, C_TQ, LANES), lambda b, j, i: (b, i, j)),
            pl.BlockSpec((None, seq, LANES), lambda b, j, i: (b, 0, N_PAIRS + j)),
            pl.BlockSpec((None, seq, LANES), lambda b, j, i: (b, 0, 2 * N_PAIRS + j)),
            _const_spec((C_TK, C_TK)),
        ],
        out_specs=pl.BlockSpec((None, C_TQ, LANES), lambda b, j, i: (b, i, j)),
        out_shape=jax.ShapeDtypeStruct((bsz, seq, D_MODEL), BF16),
        compiler_params=_cparams(("parallel", "parallel", "arbitrary")),
        name="attn_stick",
    )(qkv, qkv, qkv, tri)


def _attn_b_body(qt_ref, qi_ref, wi_ref, k_ref, vt_ref, ki_ref, o_ref, key_ref, bias_ref, *, topk):
    i = pl.program_id(1)
    nkb = i + 1
    lane_q = lax.broadcasted_iota(jnp.int32, (1, B_TQ), 1)
    limit = ((i * B_TQ + lane_q) // CHUNK + 1) * CHUNK
    row_k = lax.broadcasted_iota(jnp.int32, (B_TK, B_TQ), 0)
    qi = qi_ref[...]
    wi = wi_ref[...] * (IDX_HEADS ** -0.5)

    def score_block(kb, carry):
        r0 = pl.multiple_of(kb * B_TK, B_TK)
        logits = _dot_nt(ki_ref[pl.ds(r0, B_TK), :], qi)
        score = jnp.zeros((B_TK, B_TQ), F32)
        for h in range(IDX_HEADS):
            score = score + wi[h:h + 1, :] * jnp.maximum(logits[:, h * B_TQ:(h + 1) * B_TQ], 0.0)
        bits = pltpu.bitcast(score, jnp.int32)
        okey = bits ^ ((bits >> 31) & jnp.int32(0x7FFFFFFF))
        adm = (row_k + kb * B_TK) < limit
        key_ref[pl.ds(r0, B_TK), :] = jnp.where(adm, okey, jnp.int32(-(2 ** 31)))
        return carry

    lax.fori_loop(0, nkb, score_block, 0)

    def count(pred_fn):
        def body(kb, c):
            r0 = pl.multiple_of(kb * B_TK, B_TK)
            hit = pred_fn(key_ref[pl.ds(r0, B_TK), :], kb)
            return c + jnp.sum(jnp.where(hit, 1, 0), axis=0, keepdims=True)
        return lax.fori_loop(0, nkb, body, jnp.zeros((1, B_TQ), jnp.int32))

    sign = jnp.int32(-(2 ** 31))

    def bit_step(n, t):
        cand = t | (jnp.int32(1) << (31 - n))
        cnt = count(lambda kk, kb: kk >= (cand ^ sign))
        return jnp.where(cnt >= topk, cand, t)

    thr = lax.fori_loop(0, 32, bit_step, jnp.zeros((1, B_TQ), jnp.int32)) ^ sign
    n_gt = count(lambda kk, kb: kk > thr)
    n_ge = count(lambda kk, kb: kk >= thr)
    need = topk - n_gt

    def tie_search():
        def idx_step(n, j):
            cand = j | (jnp.int32(1) << (15 - n))
            cnt = count(lambda kk, kb: (kk == thr) & ((row_k + kb * B_TK) < cand))
            return jnp.where(cnt < need, cand, j)
        return lax.fori_loop(0, 16, idx_step, jnp.zeros((1, B_TQ), jnp.int32))

    tie_hi = lax.cond(jnp.max(n_ge) > topk, tie_search,
                      lambda: jnp.full((1, B_TQ), 2 ** 30, jnp.int32))

    def bias_block(kb, carry):
        r0 = pl.multiple_of(kb * B_TK, B_TK)
        kk = key_ref[pl.ds(r0, B_TK), :]
        pos = row_k + kb * B_TK
        sel = (kk > thr) | ((kk == thr) & (pos <= tie_hi))
        sel = sel & (pos < limit)
        bias_ref[pl.ds(r0, B_TK), :] = jnp.where(sel, 0.0, NEG)
        return carry

    lax.fori_loop(0, nkb, bias_block, 0)

    sub = lax.broadcasted_iota(jnp.int32, (LANES, 1), 0)
    for j in range(N_PAIRS):
        qt = qt_ref[j * LANES:(j + 1) * LANES, :]
        for half in range(2):
            qh = jnp.where((sub < HEAD_DIM) == (half == 0), qt, jnp.zeros_like(qt))
            d0 = j * LANES + half * HEAD_DIM

            def kv_step(kb, carry, qh=qh, d0=d0, j=j):
                m_run, l_run, acc = carry
                r0 = pl.multiple_of(kb * B_TK, B_TK)
                s = _dot(k_ref[pl.ds(r0, B_TK), j * LANES:(j + 1) * LANES], qh)
                s = s + bias_ref[pl.ds(r0, B_TK), :]
                m_new = jnp.maximum(m_run, jnp.max(s, axis=0, keepdims=True))
                alpha = jnp.exp(m_run - m_new)
                p = jnp.exp(s - m_new)
                l_new = alpha * l_run + jnp.sum(p, axis=0, keepdims=True)
                pv = _dot(vt_ref[d0:d0 + HEAD_DIM, pl.ds(r0, B_TK)], p.astype(BF16))
                return m_new, l_new, alpha * acc + pv

            init = (jnp.full((1, B_TQ), NEG, F32), jnp.zeros((1, B_TQ), F32),
                    jnp.zeros((HEAD_DIM, B_TQ), F32))
            _, l_fin, acc = lax.fori_loop(0, nkb, kv_step, init)
            o_ref[d0:d0 + HEAD_DIM, :] = (acc * (1.0 / l_fin)).astype(o_ref.dtype)


def _attn_b(qt, qi, wi, k, vt, ki, bsz, seq, topk):
    nblk = seq // B_TQ
    one = pl.Buffered(1)
    return pl.pallas_call(
        functools.partial(_attn_b_body, topk=topk),
        grid=(bsz, nblk),
        in_specs=[
            pl.BlockSpec((None, None, D_MODEL, B_TQ), lambda b, i: (b, i, 0, 0)),
            pl.BlockSpec((None, None, IDX_HEADS * B_TQ, IDX_DIM), lambda b, i: (b, i, 0, 0)),
            pl.BlockSpec((None, None, IDX_HEADS, B_TQ), lambda b, i: (b, i, 0, 0)),
            pl.BlockSpec((None, seq, D_MODEL), lambda b, i: (b, 0, 0), pipeline_mode=one),
            pl.BlockSpec((None, D_MODEL, seq), lambda b, i: (b, 0, 0), pipeline_mode=one),
            pl.BlockSpec((None, seq, IDX_DIM), lambda b, i: (b, 0, 0), pipeline_mode=one),
        ],
        out_specs=pl.BlockSpec((None, None, D_MODEL, B_TQ), lambda b, i: (b, i, 0, 0)),
        out_shape=jax.ShapeDtypeStruct((bsz, nblk, D_MODEL, B_TQ), BF16),
        scratch_shapes=[pltpu.VMEM((seq, B_TQ), jnp.int32), pltpu.VMEM((seq, B_TQ), F32)],
        compiler_params=_cparams(("parallel", "arbitrary")),
        name="attn_topk",
    )(qt, qi, wi, k, vt, ki)


def _mlp_body(x_ref, o_ref, wo_ref, g_ref, win_ref, cw_ref, cb_ref, wd_ref, y_ref, carry_ref, *, blocks_per_seq):
    i = pl.program_id(0)

    @pl.when((i % blocks_per_seq) == 0)
    def _():
        carry_ref[...] = jnp.zeros_like(carry_ref)

    x1 = x_ref[...] + _dot(o_ref[...], wo_ref[...])
    y_ref[...] = x1
    hb = _rms(x1, g_ref[...]).astype(BF16)
    row = lax.broadcasted_iota(jnp.int32, (8, F_TILE), 0)

    def shifted(a, prev, n):
        r = pltpu.roll(a, n, 0)
        top = jnp.where(row < n, pltpu.roll(prev, n, 0), r[:8])
        return jnp.concatenate([top, r[8:]], axis=0)

    for f0 in range(0, D_FF, F_TILE):
        branches = []
        for c0 in (f0, D_FF + f0):
            a = _dot(hb, win_ref[:, c0:c0 + F_TILE])
            prev = carry_ref[:, c0:c0 + F_TILE]
            carry_ref[:, c0:c0 + F_TILE] = a[TM - 8:, :]
            branches.append(cb_ref[:, c0:c0 + F_TILE]
                            + shifted(a, prev, 2) * cw_ref[0:1, c0:c0 + F_TILE]
                            + shifted(a, prev, 1) * cw_ref[1:2, c0:c0 + F_TILE]
                            + a * cw_ref[2:3, c0:c0 + F_TILE])
        gate, up = branches
        act = (gate * (1.0 / (1.0 + jnp.exp(-gate))) * up).astype(BF16)
        y_ref[...] += _dot(act, wd_ref[f0:f0 + F_TILE, :])


def _mlp(x, o, wo, g, w_in, conv_w, conv_b, w_down, seq):
    m = x.shape[0]
    return pl.pallas_call(
        functools.partial(_mlp_body, blocks_per_seq=seq // TM),
        grid=(m // TM,),
        in_specs=[
            pl.BlockSpec((TM, D_MODEL), lambda i: (i, 0)),
            pl.BlockSpec((TM, D_MODEL), lambda i: (i, 0)),
            _const_spec((D_MODEL, D_MODEL)),
            _const_spec((1, D_MODEL)),
            _const_spec((D_MODEL, 2 * D_FF)),
            _const_spec((CONV_W, 2 * D_FF)),
            _const_spec((1, 2 * D_FF)),
            _const_spec((D_FF, D_MODEL)),
        ],
        out_specs=pl.BlockSpec((TM, D_MODEL), lambda i: (i, 0)),
        out_shape=jax.ShapeDtypeStruct((m, D_MODEL), F32),
        scratch_shapes=[pltpu.VMEM((8, 2 * D_FF), F32)],
        compiler_params=_cparams(("arbitrary",)),
        name="oproj_conv_mlp",
    )(x, o, wo, g.reshape(1, D_MODEL), w_in, conv_w, conv_b.reshape(1, 2 * D_FF), w_down)


def _rope_tables(seq):
    inv = ROPE_THETA ** (-jnp.arange(0, HEAD_DIM, 2, dtype=F32) / HEAD_DIM)
    ang = jnp.arange(seq, dtype=F32)[:, None] * inv[None, :]
    cos, sin = jnp.cos(ang), jnp.sin(ang)
    return jnp.tile(cos, (1, 4)), jnp.tile(jnp.concatenate([-sin, sin], axis=1), (1, 2))


def _pair_gain(gq, gk):
    return jnp.stack([jnp.tile(gq.astype(F32), 2), jnp.tile(gk.astype(F32), 2)])


def kernel(x, norm1_g, norm2_g, a_w_qkv, a_q_norm, a_k_norm, a_rel_bias, a_w_o,
           b_w_in, b_q_norm, b_k_norm, b_w_o, c_w_qkv, c_w_o,
           ffn_w_in, ffn_conv_w, ffn_conv_b, ffn_w_down):
    bsz, seq, _ = x.shape
    depth = norm1_g.shape[0]
    rows = bsz * seq
    blk = jnp.arange(LANES) // HEAD_DIM
    bd = (blk[:, None] == blk[None, :]).astype(BF16)
    tri = (jnp.arange(C_TK)[:, None] >= jnp.arange(C_TK)[None, :]).astype(BF16)
    rope_tabs = _rope_tables(seq)
    no_gain = jnp.ones((2, LANES), F32)
    plain = (None, None, 1.0)
    q_scale = HEAD_DIM ** -0.5

    ia = ib = ic = 0
    for layer in range(depth):
        kind = layer % 3
        if kind == 0:
            gains = _pair_gain(a_q_norm[ia], a_k_norm[ia])
            qkv = _project(x, norm1_g[layer], a_w_qkv[ia].astype(BF16), [plain] * (3 * N_PAIRS),
                           no_gain, bd, lead_rows=PAD)
            o = _attn_a(qkv, _band_bias(a_rel_bias[ia]), bd, gains, bsz, seq)
            w_o = a_w_o[ia]
            ia += 1
        elif kind == 1:
            gains = _pair_gain(b_q_norm[ib], b_k_norm[ib])
            n_in = b_w_in.shape[-1]
            n_main = 3 * D_MODEL + IDX_HEADS * IDX_DIM
            w = jnp.pad(b_w_in[ib], ((0, 0), (0, n_main + LANES - n_in))).astype(BF16)
            modes = ([(0, "all", q_scale)] * N_PAIRS + [(1, "all", 1.0)] * N_PAIRS + [plain] * N_PAIRS
                     + [(None, "all", IDX_DIM ** -0.5)] * (IDX_HEADS * IDX_DIM // LANES)
                     + [(None, "low_head", 1.0)])
            proj, tail = _project(x, norm1_g[layer], w, modes, gains, bd, rope_tabs=rope_tabs, aux=True)
            nblk = seq // B_TQ
            q = proj[..., :D_MODEL]
            k = proj[..., D_MODEL:2 * D_MODEL]
            v = proj[..., 2 * D_MODEL:3 * D_MODEL]
            qi = proj[..., 3 * D_MODEL:n_main]
            ki = proj[..., n_main:n_main + IDX_DIM]
            wi = tail[..., IDX_DIM:IDX_DIM + IDX_HEADS]
            qt = q.reshape(bsz, nblk, B_TQ, D_MODEL).swapaxes(2, 3)
            vt = v.swapaxes(1, 2)
            qi_t = (qi.reshape(bsz, nblk, B_TQ, IDX_HEADS, IDX_DIM).swapaxes(2, 3)
                    .reshape(bsz, nblk, IDX_HEADS * B_TQ, IDX_DIM))
            wi_t = wi.reshape(bsz, nblk, B_TQ, IDX_HEADS).swapaxes(2, 3)
            ot = _attn_b(qt, qi_t, wi_t, k, vt, ki, bsz, seq, min(TOPK_MAX, seq // 4))
            o = ot.swapaxes(2, 3).reshape(bsz, seq, D_MODEL)
            w_o = b_w_o[ib]
            ib += 1
        else:
            modes = [(None, None, q_scale)] * N_PAIRS + [plain] * (2 * N_PAIRS)
            qkv = _project(x, norm1_g[layer], c_w_qkv[ic].astype(BF16), modes, no_gain, bd)
            o = _attn_c(qkv, tri, bsz, seq)
            w_o = c_w_o[ic]
            ic += 1
        x = _mlp(x.reshape(rows, D_MODEL), o.reshape(rows, D_MODEL), w_o.astype(BF16), norm2_g[layer],
                 ffn_w_in[layer].astype(BF16), ffn_conv_w[layer], ffn_conv_b[layer],
                 ffn_w_down[layer].astype(BF16), seq).reshape(bsz, seq, D_MODEL)
    return x
```

```python
import functools

import jax
import jax.numpy as jnp
from jax import lax
from jax.experimental import pallas as pl
from jax.experimental.pallas import tpu as pltpu

D_MODEL = 1024
N_HEADS = 16
HEAD_DIM = 64
N_PAIRS = N_HEADS // 2
LANES = 128
CHUNK = 64
LEFT_CHUNKS = 8
PAD = LEFT_CHUNKS * CHUNK
MAX_REL = 256
IDX_HEADS = 8
IDX_DIM = 64
TOPK_MAX = 256
D_FF = 2816
CONV_W = 3
ROPE_THETA = 10000.0
EPS = 1e-6
NEG = -1e30

TM = 512
A_TQ = 128
A_W = A_TQ + PAD
A_GROUP = 4
C_GROUP = 2
LOG2E = 1.4426950408889634
B_TQ = 128
B_TK = 128
B_SCAN = 512
C_TQ = 256
C_TK = 256
F_TILE = 256
VMEM_LIMIT = 56 * 1024 * 1024

BF16 = jnp.bfloat16
F32 = jnp.float32


def _cparams(sem):
    return pltpu.CompilerParams(dimension_semantics=sem, vmem_limit_bytes=VMEM_LIMIT)


def _const_spec(shape):
    nd = len(shape)
    return pl.BlockSpec(shape, lambda *_: (0,) * nd, pipeline_mode=pl.Buffered(1))


def _dot(a, b):
    return jnp.dot(a, b, preferred_element_type=F32)


def _dot_nt(a, b):
    return lax.dot_general(a, b, (((1,), (1,)), ((), ())), preferred_element_type=F32)


def _rms(x, g):
    return x * lax.rsqrt(jnp.mean(x * x, axis=-1, keepdims=True) + EPS) * g


def _lane_half_masks(dtype):
    lane = lax.broadcasted_iota(jnp.int32, (1, LANES), 1)
    lo = (lane < HEAD_DIM).astype(dtype)
    return lo, (1 - lo).astype(dtype)


def _head_norm(y, bd, gain):
    ss = _dot((y * y).astype(BF16), bd)
    return y * lax.rsqrt(ss * (1.0 / HEAD_DIM) + EPS) * gain


def _rope(y, cos_t, sin_t):
    lane = lax.broadcasted_iota(jnp.int32, y.shape, 1)
    first = (lane % HEAD_DIM) < (HEAD_DIM // 2)
    partner = jnp.where(first, pltpu.roll(y, LANES - HEAD_DIM // 2, 1), pltpu.roll(y, HEAD_DIM // 2, 1))
    return y * cos_t + partner * sin_t


def _proj_body(*refs, modes, use_rope, n_aux, lead):
    it = iter(refs)
    x_ref, g_ref, w_ref, bd_ref, gain_ref = next(it), next(it), next(it), next(it), next(it)
    cos_ref = sin_ref = None
    if use_rope:
        cos_ref, sin_ref = next(it), next(it)
    o_ref = next(it)
    aux_ref = next(it) if n_aux else None
    n_out = 128 * len(modes)

    def compute():
        hb = _rms(x_ref[...], g_ref[...]).astype(BF16)
        bd = bd_ref[...]
        for c0 in range(0, n_out, 512):
            wd = min(512, n_out - c0)
            y = _dot(hb, w_ref[:, c0:c0 + wd])
            for t0 in range(0, wd, LANES):
                norm_idx, rope, scale = modes[(c0 + t0) // LANES]
                yt = y[:, t0:t0 + LANES]
                if norm_idx is not None:
                    yt = _head_norm(yt, bd, gain_ref[norm_idx:norm_idx + 1, :])
                if rope == "all":
                    yt = _rope(yt, cos_ref[...], sin_ref[...])
                elif rope == "low_head":
                    lane = lax.broadcasted_iota(jnp.int32, yt.shape, 1)
                    yt = jnp.where(lane < HEAD_DIM, _rope(yt, cos_ref[...], sin_ref[...]), yt)
                if scale != 1.0:
                    yt = yt * scale
                o_ref[:, c0 + t0:c0 + t0 + LANES] = yt.astype(o_ref.dtype)
                if n_aux and (c0 + t0) // LANES == len(modes) - 1:
                    aux_ref[...] = yt

    if lead:
        @pl.when(pl.program_id(1) < lead)
        def _():
            o_ref[...] = jnp.zeros_like(o_ref)

        pl.when(pl.program_id(1) >= lead)(compute)
    else:
        compute()


def _project(x, g, w, modes, gains, bd, rope_tabs=None, aux=False, lead_rows=0):
    bsz, seq, _ = x.shape
    n_out = LANES * len(modes)
    lead = lead_rows // TM
    nblk = seq // TM
    in_specs = [
        pl.BlockSpec((None, TM, D_MODEL), lambda b, i: (b, jnp.maximum(i - lead, 0), 0)),
        _const_spec((1, D_MODEL)),
        _const_spec((D_MODEL, n_out)),
        _const_spec((LANES, LANES)),
        _const_spec(gains.shape),
    ]
    args = [x, g.reshape(1, D_MODEL), w, bd, gains]
    if rope_tabs is not None:
        in_specs += [pl.BlockSpec((TM, LANES), lambda b, i: (i, 0))] * 2
        args += list(rope_tabs)
    out_shape = [jax.ShapeDtypeStruct((bsz, lead_rows + seq, n_out), BF16)]
    out_specs = [pl.BlockSpec((None, TM, n_out), lambda b, i: (b, i, 0))]
    if aux:
        out_shape.append(jax.ShapeDtypeStruct((bsz, seq, LANES), F32))
        out_specs.append(pl.BlockSpec((None, TM, LANES), lambda b, i: (b, i, 0)))
    res = pl.pallas_call(
        functools.partial(_proj_body, modes=tuple(modes), use_rope=rope_tabs is not None,
                          n_aux=int(aux), lead=lead),
        grid=(bsz, lead + nblk),
        in_specs=in_specs,
        out_specs=out_specs,
        out_shape=out_shape,
        compiler_params=_cparams(("parallel", "arbitrary")),
        name="norm_proj",
    )(*args)
    return res if aux else res[0]


def _attn_a_body(q_ref, k_ref, v_ref, bias_ref, bd_ref, gain_ref, o_ref, kn_ref):
    i = pl.program_id(2)
    bd = bd_ref[...]
    n_rows = kn_ref.shape[0]

    @pl.when(i == 0)
    def _():
        def norm_rows(c, carry):
            r0 = pl.multiple_of(c * TM, TM)
            for g in range(A_GROUP):
                kk = k_ref[pl.ds(r0, TM), g * LANES:(g + 1) * LANES].astype(F32)
                kn_ref[pl.ds(r0, TM), g * LANES:(g + 1) * LANES] = (
                    _head_norm(kk, bd, gain_ref[1:2, :]).astype(BF16))
            return carry
        lax.fori_loop(0, n_rows // TM, norm_rows, 0)

    lo, hi = _lane_half_masks(F32)
    w0 = pl.multiple_of(i * A_TQ, A_TQ)
    col = lax.broadcasted_iota(jnp.int32, (1, A_W), 1)
    valid = (col + (i * A_TQ - PAD)) >= 0
    for g in range(A_GROUP):
        lanes = slice(g * LANES, (g + 1) * LANES)
        q = _head_norm(q_ref[:, lanes].astype(F32), bd, gain_ref[0:1, :]) * (LOG2E * HEAD_DIM ** -0.5)
        kw = kn_ref[pl.ds(w0, A_W), lanes]
        vw = v_ref[pl.ds(w0, A_W), lanes]
        out = None
        for half, m in enumerate((lo, hi)):
            s = _dot_nt((q * m).astype(BF16), kw) + bias_ref[2 * g + half]
            s = jnp.where(valid, s, NEG)
            p = jnp.exp2(s - jnp.max(s, axis=-1, keepdims=True))
            l = jnp.sum(p, axis=-1, keepdims=True)
            o = _dot(p.astype(BF16), vw * m.astype(BF16)) * (1.0 / l)
            out = o if out is None else out + o
        o_ref[:, lanes] = out.astype(o_ref.dtype)


def _attn_a(qkv, bias, bd, gains, bsz, seq):
    rows = PAD + seq
    width = A_GROUP * LANES
    groups = N_PAIRS // A_GROUP
    return pl.pallas_call(
        _attn_a_body,
        grid=(bsz, groups, seq // A_TQ),
        in_specs=[
            pl.BlockSpec((None, A_TQ, width), lambda b, j, i: (b, i + PAD // A_TQ, j)),
            pl.BlockSpec((None, rows, width), lambda b, j, i: (b, 0, groups + j)),
            pl.BlockSpec((None, rows, width), lambda b, j, i: (b, 0, 2 * groups + j)),
            pl.BlockSpec((2 * A_GROUP, A_TQ, A_W), lambda b, j, i: (j, 0, 0)),
            _const_spec((LANES, LANES)),
            _const_spec(gains.shape),
        ],
        out_specs=pl.BlockSpec((None, A_TQ, width), lambda b, j, i: (b, i, j)),
        out_shape=jax.ShapeDtypeStruct((bsz, seq, D_MODEL), BF16),
        scratch_shapes=[pltpu.VMEM((rows, width), BF16)],
        compiler_params=_cparams(("parallel", "parallel", "arbitrary")),
        name="attn_band",
    )(qkv, qkv, qkv, bias, bd, gains)


def _band_bias(rel_bias):
    r = jnp.arange(A_TQ)[:, None]
    c = jnp.arange(A_W)[None, :]
    rel = jnp.clip(r - c + PAD, -MAX_REL, MAX_REL) + MAX_REL
    in_band = (c // CHUNK >= r // CHUNK) & (c // CHUNK <= r // CHUNK + LEFT_CHUNKS)
    return jnp.where(in_band[None], rel_bias.astype(F32)[:, rel] * LOG2E, NEG)


def _softplus(z):
    return jnp.maximum(z, 0.0) + jnp.log(1.0 + jnp.exp(-jnp.abs(z)))


def _attn_c_body(q_ref, k_ref, v_ref, tri_ref, o_ref, acc_ref):
    i = pl.program_id(2)
    tri = tri_ref[...]
    masks = _lane_half_masks(BF16)
    q = q_ref[...]
    chains = [(g, m, q[:, g * LANES:(g + 1) * LANES] * m) for g in range(C_GROUP) for m in masks]
    row = lax.broadcasted_iota(jnp.int32, (C_TQ, C_TK), 0)
    colk = lax.broadcasted_iota(jnp.int32, (C_TQ, C_TK), 1)
    causal = colk < row

    def block(kb, runs, diag):
        r0 = pl.multiple_of(kb * C_TK, C_TK)
        pv = [None] * C_GROUP
        new_runs = []
        for (g, m, qh), run in zip(chains, runs):
            lanes = slice(g * LANES, (g + 1) * LANES)
            z = _dot_nt(qh, k_ref[pl.ds(r0, C_TK), lanes])
            sp = _softplus(z)
            if diag:
                sp = jnp.where(causal, sp, 0.0)
            tail = _dot(sp.astype(BF16), tri)
            a = jnp.exp(z - (tail + run))
            if diag:
                a = jnp.where(causal, a, 0.0)
            contrib = _dot(a.astype(BF16), v_ref[pl.ds(r0, C_TK), lanes] * m)
            pv[g] = contrib if pv[g] is None else pv[g] + contrib
            new_runs.append(run + jnp.sum(sp, axis=-1, keepdims=True))
        return pv, tuple(new_runs)

    zero_run = jnp.zeros((C_TQ, 1), F32)
    pv, runs = block(i, (zero_run,) * len(chains), True)
    for g in range(C_GROUP):
        acc_ref[:, g * LANES:(g + 1) * LANES] = pv[g]

    def step(n, runs):
        pv, runs = block(i - 1 - n, runs, False)
        for g in range(C_GROUP):
            acc_ref[:, g * LANES:(g + 1) * LANES] += pv[g]
        return runs

    lax.fori_loop(0, i, step, runs)
    o_ref[...] = acc_ref[...].astype(o_ref.dtype)


def _attn_c(qkv, tri, bsz, seq):
    width = C_GROUP * LANES
    groups = N_PAIRS // C_GROUP
    return pl.pallas_call(
        _attn_c_body,
        grid=(bsz, groups, seq // C_TQ),
        in_specs=[
            pl.BlockSpec((None, C_TQ, width), lambda b, j, i: (b, i, j)),
            pl.BlockSpec((None, seq, width), lambda b, j, i: (b, 0, groups + j)),
            pl.BlockSpec((None, seq, width), lambda b, j, i: (b, 0, 2 * groups + j)),
            _const_spec((C_TK, C_TK)),
        ],
        out_specs=pl.BlockSpec((None, C_TQ, width), lambda b, j, i: (b, i, j)),
        out_shape=jax.ShapeDtypeStruct((bsz, seq, D_MODEL), BF16),
        scratch_shapes=[pltpu.VMEM((C_TQ, width), F32)],
        compiler_params=_cparams(("parallel", "parallel", "arbitrary")),
        name="attn_stick",
    )(qkv, qkv, qkv, tri)


def _attn_b_body(qt_ref, qi_ref, wi_ref, k_ref, vt_ref, ki_ref, o_ref,
                 key_ref, bias_ref, qm_ref, m_ref, l_ref, acc_ref, *, topk):
    i = pl.program_id(1)
    nkb = i + 1
    lane_q = lax.broadcasted_iota(jnp.int32, (1, B_TQ), 1)
    limit = ((i * B_TQ + lane_q) // CHUNK + 1) * CHUNK
    row_k = lax.broadcasted_iota(jnp.int32, (B_TK, B_TQ), 0)
    qi = qi_ref[...]
    wi = wi_ref[...] * (IDX_HEADS ** -0.5)

    def score_group(sb, carry):
        for u in range(B_SCAN // B_TK):
            r0 = pl.multiple_of(sb * B_SCAN + u * B_TK, B_TK)
            logits = _dot_nt(ki_ref[pl.ds(r0, B_TK), :], qi)
            score = jnp.zeros((B_TK, B_TQ), F32)
            for h in range(IDX_HEADS):
                score = score + wi[h:h + 1, :] * jnp.maximum(logits[:, h * B_TQ:(h + 1) * B_TQ], 0.0)
            bits = pltpu.bitcast(score, jnp.int32)
            okey = bits ^ ((bits >> 31) & jnp.int32(0x7FFFFFFF))
            adm = (row_k + r0) < limit
            key_ref[pl.ds(r0, B_TK), :] = jnp.where(adm, okey, jnp.int32(-(2 ** 31)))
        return carry

    n_scan = (i + B_SCAN // B_TK) // (B_SCAN // B_TK)
    lax.fori_loop(0, n_scan, score_group, 0)
    row_s = lax.broadcasted_iota(jnp.int32, (B_SCAN, B_TQ), 0)

    def count(pred_fn):
        def body(sb, c):
            r0 = pl.multiple_of(sb * B_SCAN, B_SCAN)
            hit = pred_fn(key_ref[pl.ds(r0, B_SCAN), :], row_s + sb * B_SCAN)
            return c + jnp.sum(jnp.where(hit, 1, 0).reshape(B_SCAN // 8, 8, B_TQ), axis=0)
        part = lax.fori_loop(0, n_scan, body, jnp.zeros((8, B_TQ), jnp.int32))
        return jnp.sum(part, axis=0, keepdims=True)

    sign = jnp.int32(-(2 ** 31))

    def bit_step(n, t):
        cand = t | (jnp.int32(1) << (31 - n))
        cnt = count(lambda kk, pos: kk >= (cand ^ sign))
        return jnp.where(cnt >= topk, cand, t)

    thr = lax.fori_loop(0, 32, bit_step, jnp.zeros((1, B_TQ), jnp.int32)) ^ sign
    n_gt = count(lambda kk, pos: kk > thr)
    n_ge = count(lambda kk, pos: kk >= thr)
    need = topk - n_gt

    def tie_search():
        def idx_step(n, j):
            cand = j | (jnp.int32(1) << (15 - n))
            cnt = count(lambda kk, pos: (kk == thr) & (pos < cand))
            return jnp.where(cnt < need, cand, j)
        return lax.fori_loop(0, 16, idx_step, jnp.zeros((1, B_TQ), jnp.int32))

    tie_hi = lax.cond(jnp.max(n_ge) > topk, tie_search,
                      lambda: jnp.full((1, B_TQ), 2 ** 30, jnp.int32))

    def bias_block(kb, carry):
        r0 = pl.multiple_of(kb * B_TK, B_TK)
        kk = key_ref[pl.ds(r0, B_TK), :]
        pos = row_k + kb * B_TK
        sel = (kk > thr) | ((kk == thr) & (pos <= tie_hi))
        sel = sel & (pos < limit)
        bias_ref[pl.ds(r0, B_TK), :] = jnp.where(sel, 0.0, NEG)
        return carry

    lax.fori_loop(0, nkb, bias_block, 0)

    sub = lax.broadcasted_iota(jnp.int32, (LANES, 1), 0)
    for j in range(N_PAIRS):
        qt = qt_ref[j * LANES:(j + 1) * LANES, :]
        for half in range(2):
            qm_ref[2 * j + half] = jnp.where((sub < HEAD_DIM) == (half == 0), qt, jnp.zeros_like(qt))
    m_ref[...] = jnp.full(m_ref.shape, NEG, F32)
    l_ref[...] = jnp.zeros(l_ref.shape, F32)
    acc_ref[...] = jnp.zeros(acc_ref.shape, F32)

    def kv_step(kb, carry):
        r0 = pl.multiple_of(kb * B_TK, B_TK)
        bias = bias_ref[pl.ds(r0, B_TK), :]
        for j in range(N_PAIRS):
            k_blk = k_ref[pl.ds(r0, B_TK), j * LANES:(j + 1) * LANES]
            for half in range(2):
                h = 2 * j + half
                d0 = h * HEAD_DIM
                s = _dot(k_blk, qm_ref[h]) + bias
                m_old = m_ref[h:h + 1, :]
                m_new = jnp.maximum(m_old, jnp.max(s, axis=0, keepdims=True))
                alpha = jnp.exp2(m_old - m_new)
                p = jnp.exp2(s - m_new)
                l_ref[h:h + 1, :] = alpha * l_ref[h:h + 1, :] + jnp.sum(p, axis=0, keepdims=True)
                m_ref[h:h + 1, :] = m_new
                pv = _dot(vt_ref[d0:d0 + HEAD_DIM, pl.ds(r0, B_TK)], p.astype(BF16))
                acc_ref[d0:d0 + HEAD_DIM, :] = alpha * acc_ref[d0:d0 + HEAD_DIM, :] + pv
        return carry

    lax.fori_loop(0, nkb, kv_step, 0)
    for h in range(N_HEADS):
        d0 = h * HEAD_DIM
        o_ref[d0:d0 + HEAD_DIM, :] = (acc_ref[d0:d0 + HEAD_DIM, :]
                                      * (1.0 / l_ref[h:h + 1, :])).astype(o_ref.dtype)


def _attn_b(qt, qi, wi, k, vt, ki, bsz, seq, topk):
    nblk = seq // B_TQ
    one = pl.Buffered(1)
    return pl.pallas_call(
        functools.partial(_attn_b_body, topk=topk),
        grid=(bsz, nblk),
        in_specs=[
            pl.BlockSpec((None, None, D_MODEL, B_TQ), lambda b, i: (b, i, 0, 0)),
            pl.BlockSpec((None, None, IDX_HEADS * B_TQ, IDX_DIM), lambda b, i: (b, i, 0, 0)),
            pl.BlockSpec((None, None, IDX_HEADS, B_TQ), lambda b, i: (b, i, 0, 0)),
            pl.BlockSpec((None, seq, D_MODEL), lambda b, i: (b, 0, 0), pipeline_mode=one),
            pl.BlockSpec((None, D_MODEL, seq), lambda b, i: (b, 0, 0), pipeline_mode=one),
            pl.BlockSpec((None, seq, IDX_DIM), lambda b, i: (b, 0, 0), pipeline_mode=one),
        ],
        out_specs=pl.BlockSpec((None, None, D_MODEL, B_TQ), lambda b, i: (b, i, 0, 0)),
        out_shape=jax.ShapeDtypeStruct((bsz, nblk, D_MODEL, B_TQ), BF16),
        scratch_shapes=[
            pltpu.VMEM((seq, B_TQ), jnp.int32),
            pltpu.VMEM((seq, B_TQ), F32),
            pltpu.VMEM((N_HEADS, LANES, B_TQ), BF16),
            pltpu.VMEM((N_HEADS, B_TQ), F32),
            pltpu.VMEM((N_HEADS, B_TQ), F32),
            pltpu.VMEM((D_MODEL, B_TQ), F32),
        ],
        compiler_params=_cparams(("parallel", "arbitrary")),
        name="attn_topk",
    )(qt, qi, wi, k, vt, ki)


def _mlp_body(x_ref, o_ref, wo_ref, g_ref, win_ref, cw_ref, cb_ref, wd_ref, y_ref, carry_ref, *, blocks_per_seq):
    i = pl.program_id(0)

    @pl.when((i % blocks_per_seq) == 0)
    def _():
        carry_ref[...] = jnp.zeros_like(carry_ref)

    x1 = x_ref[...] + _dot(o_ref[...], wo_ref[...])
    y_ref[...] = x1
    hb = _rms(x1, g_ref[...]).astype(BF16)
    row = lax.broadcasted_iota(jnp.int32, (8, F_TILE), 0)

    def shifted(a, prev, n):
        r = pltpu.roll(a, n, 0)
        top = jnp.where(row < n, pltpu.roll(prev, n, 0), r[:8])
        return jnp.concatenate([top, r[8:]], axis=0)

    for f0 in range(0, D_FF, F_TILE):
        branches = []
        for c0 in (f0, D_FF + f0):
            a = _dot(hb, win_ref[:, c0:c0 + F_TILE])
            prev = carry_ref[:, c0:c0 + F_TILE]
            carry_ref[:, c0:c0 + F_TILE] = a[TM - 8:, :]
            branches.append(cb_ref[:, c0:c0 + F_TILE]
                            + shifted(a, prev, 2) * cw_ref[0:1, c0:c0 + F_TILE]
                            + shifted(a, prev, 1) * cw_ref[1:2, c0:c0 + F_TILE]
                            + a * cw_ref[2:3, c0:c0 + F_TILE])
        gate, up = branches
        act = (gate * (1.0 / (1.0 + jnp.exp(-gate))) * up).astype(BF16)
        y_ref[...] += _dot(act, wd_ref[f0:f0 + F_TILE, :])


def _mlp(x, o, wo, g, w_in, conv_w, conv_b, w_down, seq):
    m = x.shape[0]
    return pl.pallas_call(
        functools.partial(_mlp_body, blocks_per_seq=seq // TM),
        grid=(m // TM,),
        in_specs=[
            pl.BlockSpec((TM, D_MODEL), lambda i: (i, 0)),
            pl.BlockSpec((TM, D_MODEL), lambda i: (i, 0)),
            _const_spec((D_MODEL, D_MODEL)),
            _const_spec((1, D_MODEL)),
            _const_spec((D_MODEL, 2 * D_FF)),
            _const_spec((CONV_W, 2 * D_FF)),
            _const_spec((1, 2 * D_FF)),
            _const_spec((D_FF, D_MODEL)),
        ],
        out_specs=pl.BlockSpec((TM, D_MODEL), lambda i: (i, 0)),
        out_shape=jax.ShapeDtypeStruct((m, D_MODEL), F32),
        scratch_shapes=[pltpu.VMEM((8, 2 * D_FF), F32)],
        compiler_params=_cparams(("arbitrary",)),
        name="oproj_conv_mlp",
    )(x, o, wo, g.reshape(1, D_MODEL), w_in, conv_w, conv_b.reshape(1, 2 * D_FF), w_down)


def _rope_tables(seq):
    inv = ROPE_THETA ** (-jnp.arange(0, HEAD_DIM, 2, dtype=F32) / HEAD_DIM)
    ang = jnp.arange(seq, dtype=F32)[:, None] * inv[None, :]
    cos, sin = jnp.cos(ang), jnp.sin(ang)
    return jnp.tile(cos, (1, 4)), jnp.tile(jnp.concatenate([-sin, sin], axis=1), (1, 2))


def _pair_gain(gq, gk):
    return jnp.stack([jnp.tile(gq.astype(F32), 2), jnp.tile(gk.astype(F32), 2)])


def kernel(x, norm1_g, norm2_g, a_w_qkv, a_q_norm, a_k_norm, a_rel_bias, a_w_o,
           b_w_in, b_q_norm, b_k_norm, b_w_o, c_w_qkv, c_w_o,
           ffn_w_in, ffn_conv_w, ffn_conv_b, ffn_w_down):
    bsz, seq, _ = x.shape
    depth = norm1_g.shape[0]
    rows = bsz * seq
    blk = jnp.arange(LANES) // HEAD_DIM
    bd = (blk[:, None] == blk[None, :]).astype(BF16)
    tri = (jnp.arange(C_TK)[:, None] >= jnp.arange(C_TK)[None, :]).astype(BF16)
    rope_tabs = _rope_tables(seq)
    no_gain = jnp.ones((2, LANES), F32)
    plain = (None, None, 1.0)
    q_scale = HEAD_DIM ** -0.5

    ia = ib = ic = 0
    for layer in range(depth):
        kind = layer % 3
        if kind == 0:
            gains = _pair_gain(a_q_norm[ia], a_k_norm[ia])
            qkv = _project(x, norm1_g[layer], a_w_qkv[ia].astype(BF16), [plain] * (3 * N_PAIRS),
                           no_gain, bd, lead_rows=PAD)
            o = _attn_a(qkv, _band_bias(a_rel_bias[ia]), bd, gains, bsz, seq)
            w_o = a_w_o[ia]
            ia += 1
        elif kind == 1:
            gains = _pair_gain(b_q_norm[ib], b_k_norm[ib])
            n_in = b_w_in.shape[-1]
            n_main = 3 * D_MODEL + IDX_HEADS * IDX_DIM
            w = jnp.pad(b_w_in[ib], ((0, 0), (0, n_main + LANES - n_in))).astype(BF16)
            modes = ([(0, "all", q_scale * LOG2E)] * N_PAIRS + [(1, "all", 1.0)] * N_PAIRS + [plain] * N_PAIRS
                     + [(None, "all", IDX_DIM ** -0.5)] * (IDX_HEADS * IDX_DIM // LANES)
                     + [(None, "low_head", 1.0)])
            proj, tail = _project(x, norm1_g[layer], w, modes, gains, bd, rope_tabs=rope_tabs, aux=True)
            nblk = seq // B_TQ
            q = proj[..., :D_MODEL]
            k = proj[..., D_MODEL:2 * D_MODEL]
            v = proj[..., 2 * D_MODEL:3 * D_MODEL]
            qi = proj[..., 3 * D_MODEL:n_main]
            ki = proj[..., n_main:n_main + IDX_DIM]
            wi = tail[..., IDX_DIM:IDX_DIM + IDX_HEADS]
            qt = q.reshape(bsz, nblk, B_TQ, D_MODEL).swapaxes(2, 3)
            vt = v.swapaxes(1, 2)
            qi_t = (qi.reshape(bsz, nblk, B_TQ, IDX_HEADS, IDX_DIM).swapaxes(2, 3)
                    .reshape(bsz, nblk, IDX_HEADS * B_TQ, IDX_DIM))
            wi_t = wi.reshape(bsz, nblk, B_TQ, IDX_HEADS).swapaxes(2, 3)
            ot = _attn_b(qt, qi_t, wi_t, k, vt, ki, bsz, seq, min(TOPK_MAX, seq // 4))
            o = ot.swapaxes(2, 3).reshape(bsz, seq, D_MODEL)
            w_o = b_w_o[ib]
            ib += 1
        else:
            modes = [(None, None, q_scale)] * N_PAIRS + [plain] * (2 * N_PAIRS)
            qkv = _project(x, norm1_g[layer], c_w_qkv[ic].astype(BF16), modes, no_gain, bd)
            o = _attn_c(qkv, tri, bsz, seq)
            w_o = c_w_o[ic]
            ic += 1
        x = _mlp(x.reshape(rows, D_MODEL), o.reshape(rows, D_MODEL), w_o.astype(BF16), norm2_g[layer],
                 ffn_w_in[layer].astype(BF16), ffn_conv_w[layer], ffn_conv_b[layer],
                 ffn_w_down[layer].astype(BF16), seq).reshape(bsz, seq, D_MODEL)
    return x
```

```python
import functools

import jax
import jax.numpy as jnp
from jax import lax
from jax.experimental import pallas as pl
from jax.experimental.pallas import tpu as pltpu

D_MODEL = 1024
N_HEADS = 16
HEAD_DIM = 64
N_PAIRS = N_HEADS // 2
LANES = 128
CHUNK = 64
LEFT_CHUNKS = 8
PAD = LEFT_CHUNKS * CHUNK
MAX_REL = 256
IDX_HEADS = 8
IDX_DIM = 64
TOPK_MAX = 256
D_FF = 2816
CONV_W = 3
ROPE_THETA = 10000.0
EPS = 1e-6
NEG = -1e30

TM = 512
A_TQ = 128
A_W = A_TQ + PAD
A_GROUP = 4
C_GROUP = 4
LOG2E = 1.4426950408889634
B_TQ = 128
B_TK = 128
B_SCAN = 512
MLP_TM = 1024
C_TQ = 256
C_TK = 256
F_TILE = 512
VMEM_LIMIT = 56 * 1024 * 1024

BF16 = jnp.bfloat16
F32 = jnp.float32


def _cparams(sem):
    return pltpu.CompilerParams(dimension_semantics=sem, vmem_limit_bytes=VMEM_LIMIT)


def _const_spec(shape):
    nd = len(shape)
    return pl.BlockSpec(shape, lambda *_: (0,) * nd, pipeline_mode=pl.Buffered(1))


def _dot(a, b):
    return jnp.dot(a, b, preferred_element_type=F32)


def _dot_nt(a, b):
    return lax.dot_general(a, b, (((1,), (1,)), ((), ())), preferred_element_type=F32)


def _rms(x, g):
    return x * lax.rsqrt(jnp.mean(x * x, axis=-1, keepdims=True) + EPS) * g


def _lane_half_masks(dtype):
    lane = lax.broadcasted_iota(jnp.int32, (1, LANES), 1)
    lo = (lane < HEAD_DIM).astype(dtype)
    return lo, (1 - lo).astype(dtype)


def _head_norm(y, bd, gain):
    ss = _dot((y * y).astype(BF16), bd)
    return y * lax.rsqrt(ss * (1.0 / HEAD_DIM) + EPS) * gain


def _rope(y, cos_t, sin_t):
    lane = lax.broadcasted_iota(jnp.int32, y.shape, 1)
    first = (lane % HEAD_DIM) < (HEAD_DIM // 2)
    partner = jnp.where(first, pltpu.roll(y, LANES - HEAD_DIM // 2, 1), pltpu.roll(y, HEAD_DIM // 2, 1))
    return y * cos_t + partner * sin_t


def _proj_body(*refs, modes, use_rope, n_aux, lead):
    it = iter(refs)
    x_ref, g_ref, w_ref, bd_ref, gain_ref = next(it), next(it), next(it), next(it), next(it)
    cos_ref = sin_ref = None
    if use_rope:
        cos_ref, sin_ref = next(it), next(it)
    o_ref = next(it)
    aux_ref = next(it) if n_aux else None
    n_out = 128 * len(modes)

    def compute():
        hb = _rms(x_ref[...], g_ref[...]).astype(BF16)
        bd = bd_ref[...]
        for c0 in range(0, n_out, 512):
            wd = min(512, n_out - c0)
            y = _dot(hb, w_ref[:, c0:c0 + wd])
            for t0 in range(0, wd, LANES):
                norm_idx, rope, scale = modes[(c0 + t0) // LANES]
                yt = y[:, t0:t0 + LANES]
                if norm_idx is not None:
                    yt = _head_norm(yt, bd, gain_ref[norm_idx:norm_idx + 1, :])
                if rope == "all":
                    yt = _rope(yt, cos_ref[...], sin_ref[...])
                elif rope == "low_head":
                    lane = lax.broadcasted_iota(jnp.int32, yt.shape, 1)
                    yt = jnp.where(lane < HEAD_DIM, _rope(yt, cos_ref[...], sin_ref[...]), yt)
                if scale != 1.0:
                    yt = yt * scale
                o_ref[:, c0 + t0:c0 + t0 + LANES] = yt.astype(o_ref.dtype)
                if n_aux and (c0 + t0) // LANES == len(modes) - 1:
                    aux_ref[...] = yt

    if lead:
        @pl.when(pl.program_id(1) < lead)
        def _():
            o_ref[...] = jnp.zeros_like(o_ref)

        pl.when(pl.program_id(1) >= lead)(compute)
    else:
        compute()


def _project(x, g, w, modes, gains, bd, rope_tabs=None, aux=False, lead_rows=0):
    bsz, seq, _ = x.shape
    n_out = LANES * len(modes)
    lead = lead_rows // TM
    nblk = seq // TM
    in_specs = [
        pl.BlockSpec((None, TM, D_MODEL), lambda b, i: (b, jnp.maximum(i - lead, 0), 0)),
        _const_spec((1, D_MODEL)),
        _const_spec((D_MODEL, n_out)),
        _const_spec((LANES, LANES)),
        _const_spec(gains.shape),
    ]
    args = [x, g.reshape(1, D_MODEL), w, bd, gains]
    if rope_tabs is not None:
        in_specs += [pl.BlockSpec((TM, LANES), lambda b, i: (i, 0))] * 2
        args += list(rope_tabs)
    out_shape = [jax.ShapeDtypeStruct((bsz, lead_rows + seq, n_out), BF16)]
    out_specs = [pl.BlockSpec((None, TM, n_out), lambda b, i: (b, i, 0))]
    if aux:
        out_shape.append(jax.ShapeDtypeStruct((bsz, seq, LANES), F32))
        out_specs.append(pl.BlockSpec((None, TM, LANES), lambda b, i: (b, i, 0)))
    res = pl.pallas_call(
        functools.partial(_proj_body, modes=tuple(modes), use_rope=rope_tabs is not None,
                          n_aux=int(aux), lead=lead),
        grid=(bsz, lead + nblk),
        in_specs=in_specs,
        out_specs=out_specs,
        out_shape=out_shape,
        compiler_params=_cparams(("parallel", "arbitrary")),
        name="norm_proj",
    )(*args)
    return res if aux else res[0]


def _attn_a_body(q_ref, k_ref, v_ref, bias_ref, bd_ref, gain_ref, o_ref, kn_ref):
    i = pl.program_id(2)
    bd = bd_ref[...]
    n_rows = kn_ref.shape[0]

    @pl.when(i == 0)
    def _():
        def norm_rows(c, carry):
            r0 = pl.multiple_of(c * TM, TM)
            for g in range(A_GROUP):
                kk = k_ref[pl.ds(r0, TM), g * LANES:(g + 1) * LANES].astype(F32)
                kn_ref[pl.ds(r0, TM), g * LANES:(g + 1) * LANES] = (
                    _head_norm(kk, bd, gain_ref[1:2, :]).astype(BF16))
            return carry
        lax.fori_loop(0, n_rows // TM, norm_rows, 0)

    lo, hi = _lane_half_masks(F32)
    w0 = pl.multiple_of(i * A_TQ, A_TQ)
    col = lax.broadcasted_iota(jnp.int32, (1, A_W), 1)
    valid = (col + (i * A_TQ - PAD)) >= 0
    first_head = lax.broadcasted_iota(jnp.int32, (A_TQ, LANES), 1) < HEAD_DIM
    ones = jnp.ones((A_W, LANES), BF16)
    outs = []
    for g in range(A_GROUP):
        lanes = slice(g * LANES, (g + 1) * LANES)
        q = _head_norm(q_ref[:, lanes].astype(F32), bd, gain_ref[0:1, :]) * (LOG2E * HEAD_DIM ** -0.5)
        q2 = jnp.concatenate([q * lo, q * hi], axis=0).astype(BF16)
        s = _dot_nt(q2, kn_ref[pl.ds(w0, A_W), lanes]) + bias_ref[g]
        s = jnp.where(valid, s, NEG)
        p = jnp.exp2(s - jnp.max(s, axis=-1, keepdims=True)).astype(BF16)
        v_aug = jnp.concatenate([v_ref[pl.ds(w0, A_W), lanes], ones], axis=1)
        o2 = _dot(p, v_aug)
        o2 = o2[:, :LANES] * (1.0 / o2[:, LANES:])
        outs.append(jnp.where(first_head, o2[:A_TQ], o2[A_TQ:]).astype(o_ref.dtype))
    o_ref[...] = jnp.concatenate(outs, axis=1)


def _attn_a(qkv, bias, bd, gains, bsz, seq):
    rows = PAD + seq
    width = A_GROUP * LANES
    groups = N_PAIRS // A_GROUP
    return pl.pallas_call(
        _attn_a_body,
        grid=(bsz, groups, seq // A_TQ),
        in_specs=[
            pl.BlockSpec((None, A_TQ, width), lambda b, j, i: (b, i + PAD // A_TQ, j)),
            pl.BlockSpec((None, rows, width), lambda b, j, i: (b, 0, groups + j)),
            pl.BlockSpec((None, rows, width), lambda b, j, i: (b, 0, 2 * groups + j)),
            pl.BlockSpec((A_GROUP, 2 * A_TQ, A_W), lambda b, j, i: (j, 0, 0)),
            _const_spec((LANES, LANES)),
            _const_spec(gains.shape),
        ],
        out_specs=pl.BlockSpec((None, A_TQ, width), lambda b, j, i: (b, i, j)),
        out_shape=jax.ShapeDtypeStruct((bsz, seq, D_MODEL), BF16),
        scratch_shapes=[pltpu.VMEM((rows, width), BF16)],
        compiler_params=_cparams(("parallel", "parallel", "arbitrary")),
        name="attn_band",
    )(qkv, qkv, qkv, bias, bd, gains)


def _band_bias(rel_bias):
    r = jnp.arange(A_TQ)[:, None]
    c = jnp.arange(A_W)[None, :]
    in_band = (c // CHUNK >= r // CHUNK) & (c // CHUNK <= r // CHUNK + LEFT_CHUNKS)
    period = A_W + A_TQ
    m = jnp.arange(period)
    c_minus_r = jnp.where(m < A_W, m, m - period)
    idx = jnp.clip(PAD - c_minus_r, -MAX_REL, MAX_REL) + MAX_REL
    line = rel_bias.astype(F32)[:, idx] * LOG2E
    toep = jnp.tile(line, (1, A_TQ))[:, :A_TQ * (period - 1)].reshape(-1, A_TQ, period - 1)[:, :, :A_W]
    bias = jnp.where(in_band[None], toep, NEG)
    return bias.reshape(N_PAIRS, 2 * A_TQ, A_W)


def _softplus(z):
    return jnp.maximum(z, 0.0) + jnp.log(1.0 + jnp.exp2(jnp.abs(z) * (-LOG2E)))


def _attn_c_body(q_ref, k_ref, v_ref, tri_ref, o_ref, acc_ref):
    i = pl.program_id(2)
    tri = tri_ref[...]
    lo, hi = _lane_half_masks(BF16)
    q = q_ref[...]
    q_pairs = [jnp.concatenate([q[:, g * LANES:(g + 1) * LANES] * lo,
                                q[:, g * LANES:(g + 1) * LANES] * hi], axis=0) for g in range(C_GROUP)]
    row = lax.broadcasted_iota(jnp.int32, (2 * C_TQ, C_TK), 0)
    colk = lax.broadcasted_iota(jnp.int32, (2 * C_TQ, C_TK), 1)
    causal = colk < jnp.where(row >= C_TQ, row - C_TQ, row)
    first_head = lax.broadcasted_iota(jnp.int32, (C_TQ, LANES), 1) < HEAD_DIM

    def block(kb, runs, diag):
        r0 = pl.multiple_of(kb * C_TK, C_TK)
        pv = []
        new_runs = []
        for g, (q2, run) in enumerate(zip(q_pairs, runs)):
            lanes = slice(g * LANES, (g + 1) * LANES)
            z = _dot_nt(q2, k_ref[pl.ds(r0, C_TK), lanes])
            sp = _softplus(z)
            if diag:
                sp = jnp.where(causal, sp, 0.0)
            tail = _dot(sp.astype(BF16), tri)
            a = jnp.exp(z - (tail + run))
            if diag:
                a = jnp.where(causal, a, 0.0)
            c2 = _dot(a.astype(BF16), v_ref[pl.ds(r0, C_TK), lanes])
            pv.append(jnp.where(first_head, c2[:C_TQ], c2[C_TQ:]))
            new_runs.append(run + jnp.sum(sp, axis=-1, keepdims=True))
        return pv, tuple(new_runs)

    zero_run = jnp.zeros((2 * C_TQ, 1), F32)
    pv, runs = block(i, (zero_run,) * C_GROUP, True)
    acc_ref[...] = jnp.concatenate(pv, axis=1)

    def step(n, runs):
        pv, runs = block(i - 1 - n, runs, False)
        acc_ref[...] += jnp.concatenate(pv, axis=1)
        return runs

    lax.fori_loop(0, i, step, runs)
    o_ref[...] = acc_ref[...].astype(o_ref.dtype)


def _attn_c(qkv, tri, bsz, seq):
    width = C_GROUP * LANES
    groups = N_PAIRS // C_GROUP
    return pl.pallas_call(
        _attn_c_body,
        grid=(bsz, groups, seq // C_TQ),
        in_specs=[
            pl.BlockSpec((None, C_TQ, width), lambda b, j, i: (b, i, j)),
            pl.BlockSpec((None, seq, width), lambda b, j, i: (b, 0, groups + j)),
            pl.BlockSpec((None, seq, width), lambda b, j, i: (b, 0, 2 * groups + j)),
            _const_spec((C_TK, C_TK)),
        ],
        out_specs=pl.BlockSpec((None, C_TQ, width), lambda b, j, i: (b, i, j)),
        out_shape=jax.ShapeDtypeStruct((bsz, seq, D_MODEL), BF16),
        scratch_shapes=[pltpu.VMEM((C_TQ, width), F32)],
        compiler_params=_cparams(("parallel", "parallel", "arbitrary")),
        name="attn_stick",
    )(qkv, qkv, qkv, tri)


def _attn_b_body(qt_ref, qi_ref, wi_ref, k_ref, vt_ref, ki_ref, o_ref,
                 key_ref, bias_ref, qm_ref, m_ref, l_ref, acc_ref, *, topk):
    i = pl.program_id(1)
    nkb = i + 1
    lane_q = lax.broadcasted_iota(jnp.int32, (1, B_TQ), 1)
    limit = ((i * B_TQ + lane_q) // CHUNK + 1) * CHUNK
    row_k = lax.broadcasted_iota(jnp.int32, (B_TK, B_TQ), 0)
    qi = qi_ref[...]
    wi = wi_ref[...] * (IDX_HEADS ** -0.5)

    def score_group(sb, carry):
        for u in range(B_SCAN // B_TK):
            r0 = pl.multiple_of(sb * B_SCAN + u * B_TK, B_TK)
            logits = _dot_nt(ki_ref[pl.ds(r0, B_TK), :], qi)
            score = jnp.zeros((B_TK, B_TQ), F32)
            for h in range(IDX_HEADS):
                score = score + wi[h:h + 1, :] * jnp.maximum(logits[:, h * B_TQ:(h + 1) * B_TQ], 0.0)
            bits = pltpu.bitcast(score, jnp.int32)
            okey = bits ^ ((bits >> 31) & jnp.int32(0x7FFFFFFF))
            adm = (row_k + r0) < limit
            key_ref[pl.ds(r0, B_TK), :] = jnp.where(adm, okey, jnp.int32(-(2 ** 31)))
        return carry

    n_scan = (i + B_SCAN // B_TK) // (B_SCAN // B_TK)
    lax.fori_loop(0, n_scan, score_group, 0)
    row_s = lax.broadcasted_iota(jnp.int32, (B_SCAN, B_TQ), 0)

    def count(pred_fn):
        def body(sb, c):
            r0 = pl.multiple_of(sb * B_SCAN, B_SCAN)
            hit = pred_fn(key_ref[pl.ds(r0, B_SCAN), :], row_s + sb * B_SCAN)
            return c + jnp.sum(jnp.where(hit, 1, 0).reshape(B_SCAN // 8, 8, B_TQ), axis=0)
        part = lax.fori_loop(0, n_scan, body, jnp.zeros((8, B_TQ), jnp.int32))
        return jnp.sum(part, axis=0, keepdims=True)

    sign = jnp.int32(-(2 ** 31))

    def bit_step(n, t):
        cand = t | (jnp.int32(1) << (31 - n))
        cnt = count(lambda kk, pos: kk >= (cand ^ sign))
        return jnp.where(cnt >= topk, cand, t)

    thr = lax.fori_loop(0, 32, bit_step, jnp.zeros((1, B_TQ), jnp.int32)) ^ sign
    n_gt = count(lambda kk, pos: kk > thr)
    n_ge = count(lambda kk, pos: kk >= thr)
    need = topk - n_gt

    def tie_search():
        def idx_step(n, j):
            cand = j | (jnp.int32(1) << (15 - n))
            cnt = count(lambda kk, pos: (kk == thr) & (pos < cand))
            return jnp.where(cnt < need, cand, j)
        return lax.fori_loop(0, 16, idx_step, jnp.zeros((1, B_TQ), jnp.int32))

    tie_hi = lax.cond(jnp.max(n_ge) > topk, tie_search,
                      lambda: jnp.full((1, B_TQ), 2 ** 30, jnp.int32))

    def bias_block(kb, carry):
        r0 = pl.multiple_of(kb * B_TK, B_TK)
        kk = key_ref[pl.ds(r0, B_TK), :]
        pos = row_k + kb * B_TK
        sel = (kk > thr) | ((kk == thr) & (pos <= tie_hi))
        sel = sel & (pos < limit)
        bias_ref[pl.ds(r0, B_TK), :] = jnp.where(sel, 0.0, NEG)
        return carry

    lax.fori_loop(0, nkb, bias_block, 0)

    sub = lax.broadcasted_iota(jnp.int32, (LANES, 1), 0)
    for j in range(N_PAIRS):
        qt = qt_ref[j * LANES:(j + 1) * LANES, :]
        for half in range(2):
            qm_ref[2 * j + half] = jnp.where((sub < HEAD_DIM) == (half == 0), qt, jnp.zeros_like(qt))
    m_ref[...] = jnp.full(m_ref.shape, NEG, F32)
    l_ref[...] = jnp.zeros(l_ref.shape, F32)
    acc_ref[...] = jnp.zeros(acc_ref.shape, F32)

    def kv_step(kb, carry):
        r0 = pl.multiple_of(kb * B_TK, B_TK)
        bias = bias_ref[pl.ds(r0, B_TK), :]
        for j in range(N_PAIRS):
            k_blk = k_ref[pl.ds(r0, B_TK), j * LANES:(j + 1) * LANES]
            for half in range(2):
                h = 2 * j + half
                d0 = h * HEAD_DIM
                s = _dot(k_blk, qm_ref[h]) + bias
                m_old = m_ref[h:h + 1, :]
                m_new = jnp.maximum(m_old, jnp.max(s, axis=0, keepdims=True))
                alpha = jnp.exp2(m_old - m_new)
                p = jnp.exp2(s - m_new)
                l_ref[h:h + 1, :] = alpha * l_ref[h:h + 1, :] + jnp.sum(p, axis=0, keepdims=True)
                m_ref[h:h + 1, :] = m_new
                pv = _dot(vt_ref[d0:d0 + HEAD_DIM, pl.ds(r0, B_TK)], p.astype(BF16))
                acc_ref[d0:d0 + HEAD_DIM, :] = alpha * acc_ref[d0:d0 + HEAD_DIM, :] + pv
        return carry

    lax.fori_loop(0, nkb, kv_step, 0)
    for h in range(N_HEADS):
        d0 = h * HEAD_DIM
        o_ref[d0:d0 + HEAD_DIM, :] = (acc_ref[d0:d0 + HEAD_DIM, :]
                                      * (1.0 / l_ref[h:h + 1, :])).astype(o_ref.dtype)


def _attn_b(qt, qi, wi, k, vt, ki, bsz, seq, topk):
    nblk = seq // B_TQ
    one = pl.Buffered(1)
    return pl.pallas_call(
        functools.partial(_attn_b_body, topk=topk),
        grid=(bsz, nblk),
        in_specs=[
            pl.BlockSpec((None, None, D_MODEL, B_TQ), lambda b, i: (b, i, 0, 0)),
            pl.BlockSpec((None, None, IDX_HEADS * B_TQ, IDX_DIM), lambda b, i: (b, i, 0, 0)),
            pl.BlockSpec((None, None, IDX_HEADS, B_TQ), lambda b, i: (b, i, 0, 0)),
            pl.BlockSpec((None, seq, D_MODEL), lambda b, i: (b, 0, 0), pipeline_mode=one),
            pl.BlockSpec((None, D_MODEL, seq), lambda b, i: (b, 0, 0), pipeline_mode=one),
            pl.BlockSpec((None, seq, IDX_DIM), lambda b, i: (b, 0, 0), pipeline_mode=one),
        ],
        out_specs=pl.BlockSpec((None, None, D_MODEL, B_TQ), lambda b, i: (b, i, 0, 0)),
        out_shape=jax.ShapeDtypeStruct((bsz, nblk, D_MODEL, B_TQ), BF16),
        scratch_shapes=[
            pltpu.VMEM((seq, B_TQ), jnp.int32),
            pltpu.VMEM((seq, B_TQ), F32),
            pltpu.VMEM((N_HEADS, LANES, B_TQ), BF16),
            pltpu.VMEM((N_HEADS, B_TQ), F32),
            pltpu.VMEM((N_HEADS, B_TQ), F32),
            pltpu.VMEM((D_MODEL, B_TQ), F32),
        ],
        compiler_params=_cparams(("parallel", "arbitrary")),
        name="attn_topk",
    )(qt, qi, wi, k, vt, ki)


def _mlp_body(x_ref, o_ref, wo_ref, g_ref, win_ref, cw_ref, cb_ref, wd_ref, y_ref, carry_ref, *, blocks_per_seq):
    i = pl.program_id(0)

    @pl.when((i % blocks_per_seq) == 0)
    def _():
        carry_ref[...] = jnp.zeros_like(carry_ref)

    x1 = x_ref[...] + _dot(o_ref[...], wo_ref[...])
    y_ref[...] = x1
    hb = _rms(x1, g_ref[...]).astype(BF16)
    def shifted(a, prev, n):
        row = lax.broadcasted_iota(jnp.int32, prev.shape, 0)
        r = pltpu.roll(a, n, 0)
        top = jnp.where(row < n, pltpu.roll(prev, n, 0), r[:8])
        return jnp.concatenate([top, r[8:]], axis=0)

    for f0 in range(0, D_FF, F_TILE):
        fw = min(F_TILE, D_FF - f0)
        branches = []
        for c0 in (f0, D_FF + f0):
            a = _dot(hb, win_ref[:, c0:c0 + fw])
            prev = carry_ref[:, c0:c0 + fw]
            carry_ref[:, c0:c0 + fw] = a[MLP_TM - 8:, :]
            branches.append(cb_ref[:, c0:c0 + fw]
                            + shifted(a, prev, 2) * cw_ref[0:1, c0:c0 + fw]
                            + shifted(a, prev, 1) * cw_ref[1:2, c0:c0 + fw]
                            + a * cw_ref[2:3, c0:c0 + fw])
        gate, up = branches
        act = (gate * (1.0 / (1.0 + jnp.exp(-gate))) * up).astype(BF16)
        y_ref[...] += _dot(act, wd_ref[f0:f0 + fw, :])


def _mlp(x, o, wo, g, w_in, conv_w, conv_b, w_down, seq):
    m = x.shape[0]
    return pl.pallas_call(
        functools.partial(_mlp_body, blocks_per_seq=seq // MLP_TM),
        grid=(m // MLP_TM,),
        in_specs=[
            pl.BlockSpec((MLP_TM, D_MODEL), lambda i: (i, 0)),
            pl.BlockSpec((MLP_TM, D_MODEL), lambda i: (i, 0)),
            _const_spec((D_MODEL, D_MODEL)),
            _const_spec((1, D_MODEL)),
            _const_spec((D_MODEL, 2 * D_FF)),
            _const_spec((CONV_W, 2 * D_FF)),
            _const_spec((1, 2 * D_FF)),
            _const_spec((D_FF, D_MODEL)),
        ],
        out_specs=pl.BlockSpec((MLP_TM, D_MODEL), lambda i: (i, 0)),
        out_shape=jax.ShapeDtypeStruct((m, D_MODEL), F32),
        scratch_shapes=[pltpu.VMEM((8, 2 * D_FF), F32)],
        compiler_params=_cparams(("arbitrary",)),
        name="oproj_conv_mlp",
    )(x, o, wo, g.reshape(1, D_MODEL), w_in, conv_w, conv_b.reshape(1, 2 * D_FF), w_down)


def _rope_tables(seq):
    inv = ROPE_THETA ** (-jnp.arange(0, HEAD_DIM, 2, dtype=F32) / HEAD_DIM)
    ang = jnp.arange(seq, dtype=F32)[:, None] * inv[None, :]
    cos, sin = jnp.cos(ang), jnp.sin(ang)
    return jnp.tile(cos, (1, 4)), jnp.tile(jnp.concatenate([-sin, sin], axis=1), (1, 2))


def _pair_gain(gq, gk):
    return jnp.stack([jnp.tile(gq.astype(F32), 2), jnp.tile(gk.astype(F32), 2)])


def kernel(x, norm1_g, norm2_g, a_w_qkv, a_q_norm, a_k_norm, a_rel_bias, a_w_o,
           b_w_in, b_q_norm, b_k_norm, b_w_o, c_w_qkv, c_w_o,
           ffn_w_in, ffn_conv_w, ffn_conv_b, ffn_w_down):
    bsz, seq, _ = x.shape
    depth = norm1_g.shape[0]
    rows = bsz * seq
    blk = jnp.arange(LANES) // HEAD_DIM
    bd = (blk[:, None] == blk[None, :]).astype(BF16)
    tri = (jnp.arange(C_TK)[:, None] >= jnp.arange(C_TK)[None, :]).astype(BF16)
    rope_tabs = _rope_tables(seq)
    no_gain = jnp.ones((2, LANES), F32)
    plain = (None, None, 1.0)
    q_scale = HEAD_DIM ** -0.5

    ia = ib = ic = 0
    for layer in range(depth):
        kind = layer % 3
        if kind == 0:
            gains = _pair_gain(a_q_norm[ia], a_k_norm[ia])
            qkv = _project(x, norm1_g[layer], a_w_qkv[ia].astype(BF16), [plain] * (3 * N_PAIRS),
                           no_gain, bd, lead_rows=PAD)
            o = _attn_a(qkv, _band_bias(a_rel_bias[ia]), bd, gains, bsz, seq)
            w_o = a_w_o[ia]
            ia += 1
        elif kind == 1:
            gains = _pair_gain(b_q_norm[ib], b_k_norm[ib])
            n_in = b_w_in.shape[-1]
            n_main = 3 * D_MODEL + IDX_HEADS * IDX_DIM
            w = jnp.pad(b_w_in[ib], ((0, 0), (0, n_main + LANES - n_in))).astype(BF16)
            modes = ([(0, "all", q_scale * LOG2E)] * N_PAIRS + [(1, "all", 1.0)] * N_PAIRS + [plain] * N_PAIRS
                     + [(None, "all", IDX_DIM ** -0.5)] * (IDX_HEADS * IDX_DIM // LANES)
                     + [(None, "low_head", 1.0)])
            proj, tail = _project(x, norm1_g[layer], w, modes, gains, bd, rope_tabs=rope_tabs, aux=True)
            nblk = seq // B_TQ
            q = proj[..., :D_MODEL]
            k = proj[..., D_MODEL:2 * D_MODEL]
            v = proj[..., 2 * D_MODEL:3 * D_MODEL]
            qi = proj[..., 3 * D_MODEL:n_main]
            ki = proj[..., n_main:n_main + IDX_DIM]
            wi = tail[..., IDX_DIM:IDX_DIM + IDX_HEADS]
            qt = q.reshape(bsz, nblk, B_TQ, D_MODEL).swapaxes(2, 3)
            vt = v.swapaxes(1, 2)
            qi_t = (qi.reshape(bsz, nblk, B_TQ, IDX_HEADS, IDX_DIM).swapaxes(2, 3)
                    .reshape(bsz, nblk, IDX_HEADS * B_TQ, IDX_DIM))
            wi_t = wi.reshape(bsz, nblk, B_TQ, IDX_HEADS).swapaxes(2, 3)
            ot = _attn_b(qt, qi_t, wi_t, k, vt, ki, bsz, seq, min(TOPK_MAX, seq // 4))
            o = ot.swapaxes(2, 3).reshape(bsz, seq, D_MODEL)
            w_o = b_w_o[ib]
            ib += 1
        else:
            modes = [(None, None, q_scale)] * N_PAIRS + [plain] * (2 * N_PAIRS)
            qkv = _project(x, norm1_g[layer], c_w_qkv[ic].astype(BF16), modes, no_gain, bd)
            o = _attn_c(qkv, tri, bsz, seq)
            w_o = c_w_o[ic]
            ic += 1
        x = _mlp(x.reshape(rows, D_MODEL), o.reshape(rows, D_MODEL), w_o.astype(BF16), norm2_g[layer],
                 ffn_w_in[layer].astype(BF16), ffn_conv_w[layer], ffn_conv_b[layer],
                 ffn_w_down[layer].astype(BF16), seq).reshape(bsz, seq, D_MODEL)
    return x
```

```python
import functools

import jax
import jax.numpy as jnp
from jax import lax
from jax.experimental import pallas as pl
from jax.experimental.pallas import tpu as pltpu

D_MODEL = 1024
N_HEADS = 16
HEAD_DIM = 64
N_PAIRS = N_HEADS // 2
LANES = 128
CHUNK = 64
LEFT_CHUNKS = 8
PAD = LEFT_CHUNKS * CHUNK
MAX_REL = 256
IDX_HEADS = 8
IDX_DIM = 64
TOPK_MAX = 256
D_FF = 2816
CONV_W = 3
ROPE_THETA = 10000.0
EPS = 1e-6
NEG = -1e30

TM = 512
A_TQ = 128
A_W = A_TQ + PAD
A_GROUP = 8
C_GROUP = 4
LOG2E = 1.4426950408889634
B_TQ = 128
B_TK = 128
B_SCAN = 512
B_UNROLL = 4
MLP_TM = 1024
C_TQ = 256
C_TK = 256
F_TILE = 1408
VMEM_LIMIT = 56 * 1024 * 1024

BF16 = jnp.bfloat16
F32 = jnp.float32


def _cparams(sem):
    return pltpu.CompilerParams(dimension_semantics=sem, vmem_limit_bytes=VMEM_LIMIT)


def _const_spec(shape):
    nd = len(shape)
    return pl.BlockSpec(shape, lambda *_: (0,) * nd, pipeline_mode=pl.Buffered(1))


def _dot(a, b):
    return jnp.dot(a, b, preferred_element_type=F32)


def _dot_nt(a, b):
    return lax.dot_general(a, b, (((1,), (1,)), ((), ())), preferred_element_type=F32)


def _rms(x, g):
    return x * lax.rsqrt(jnp.mean(x * x, axis=-1, keepdims=True) + EPS) * g


def _lane_half_masks(dtype):
    lane = lax.broadcasted_iota(jnp.int32, (1, LANES), 1)
    lo = (lane < HEAD_DIM).astype(dtype)
    return lo, (1 - lo).astype(dtype)


def _head_norm(y, bd, gain):
    ss = _dot((y * y).astype(BF16), bd)
    return y * lax.rsqrt(ss * (1.0 / HEAD_DIM) + EPS) * gain


def _rope(y, cos_t, sin_t):
    lane = lax.broadcasted_iota(jnp.int32, y.shape, 1)
    first = (lane % HEAD_DIM) < (HEAD_DIM // 2)
    partner = jnp.where(first, pltpu.roll(y, LANES - HEAD_DIM // 2, 1), pltpu.roll(y, HEAD_DIM // 2, 1))
    return y * cos_t + partner * sin_t


def _proj_body(*refs, modes, use_rope, n_aux, lead):
    it = iter(refs)
    x_ref, g_ref, w_ref, bd_ref, gain_ref = next(it), next(it), next(it), next(it), next(it)
    cos_ref = sin_ref = None
    if use_rope:
        cos_ref, sin_ref = next(it), next(it)
    o_ref = next(it)
    aux_ref = next(it) if n_aux else None
    n_out = 128 * len(modes)

    def compute():
        hb = _rms(x_ref[...], g_ref[...]).astype(BF16)
        bd = bd_ref[...]
        for c0 in range(0, n_out, 512):
            wd = min(512, n_out - c0)
            y = _dot(hb, w_ref[:, c0:c0 + wd])
            for t0 in range(0, wd, LANES):
                norm_idx, rope, scale = modes[(c0 + t0) // LANES]
                yt = y[:, t0:t0 + LANES]
                if norm_idx is not None:
                    yt = _head_norm(yt, bd, gain_ref[norm_idx:norm_idx + 1, :])
                if rope == "all":
                    yt = _rope(yt, cos_ref[...], sin_ref[...])
                elif rope == "low_head":
                    lane = lax.broadcasted_iota(jnp.int32, yt.shape, 1)
                    yt = jnp.where(lane < HEAD_DIM, _rope(yt, cos_ref[...], sin_ref[...]), yt)
                if scale != 1.0:
                    yt = yt * scale
                o_ref[:, c0 + t0:c0 + t0 + LANES] = yt.astype(o_ref.dtype)
                if n_aux and (c0 + t0) // LANES == len(modes) - 1:
                    aux_ref[...] = yt

    if lead:
        @pl.when(pl.program_id(1) < lead)
        def _():
            o_ref[...] = jnp.zeros_like(o_ref)

        pl.when(pl.program_id(1) >= lead)(compute)
    else:
        compute()


def _project(x, g, w, modes, gains, bd, rope_tabs=None, aux=False, lead_rows=0):
    bsz, seq, _ = x.shape
    n_out = LANES * len(modes)
    lead = lead_rows // TM
    nblk = seq // TM
    in_specs = [
        pl.BlockSpec((None, TM, D_MODEL), lambda b, i: (b, jnp.maximum(i - lead, 0), 0)),
        _const_spec((1, D_MODEL)),
        _const_spec((D_MODEL, n_out)),
        _const_spec((LANES, LANES)),
        _const_spec(gains.shape),
    ]
    args = [x, g.reshape(1, D_MODEL), w, bd, gains]
    if rope_tabs is not None:
        in_specs += [pl.BlockSpec((TM, LANES), lambda b, i: (i, 0))] * 2
        args += list(rope_tabs)
    out_shape = [jax.ShapeDtypeStruct((bsz, lead_rows + seq, n_out), BF16)]
    out_specs = [pl.BlockSpec((None, TM, n_out), lambda b, i: (b, i, 0))]
    if aux:
        out_shape.append(jax.ShapeDtypeStruct((bsz, seq, LANES), F32))
        out_specs.append(pl.BlockSpec((None, TM, LANES), lambda b, i: (b, i, 0)))
    res = pl.pallas_call(
        functools.partial(_proj_body, modes=tuple(modes), use_rope=rope_tabs is not None,
                          n_aux=int(aux), lead=lead),
        grid=(bsz, lead + nblk),
        in_specs=in_specs,
        out_specs=out_specs,
        out_shape=out_shape,
        compiler_params=_cparams(("parallel", "arbitrary")),
        name="norm_proj",
    )(*args)
    return res if aux else res[0]


def _attn_a_body(q_ref, k_ref, v_ref, bias_ref, bd_ref, gain_ref, o_ref, kn_ref):
    i = pl.program_id(2)
    bd = bd_ref[...]
    n_rows = kn_ref.shape[0]

    @pl.when(i == 0)
    def _():
        def norm_rows(c, carry):
            r0 = pl.multiple_of(c * TM, TM)
            for g in range(A_GROUP):
                kk = k_ref[pl.ds(r0, TM), g * LANES:(g + 1) * LANES].astype(F32)
                kn_ref[pl.ds(r0, TM), g * LANES:(g + 1) * LANES] = (
                    _head_norm(kk, bd, gain_ref[1:2, :]).astype(BF16))
            return carry
        lax.fori_loop(0, n_rows // TM, norm_rows, 0)

    lo, hi = _lane_half_masks(F32)
    w0 = pl.multiple_of(i * A_TQ, A_TQ)
    col = lax.broadcasted_iota(jnp.int32, (1, A_W), 1)
    valid = (col + (i * A_TQ - PAD)) >= 0
    first_head = lax.broadcasted_iota(jnp.int32, (A_TQ, LANES), 1) < HEAD_DIM
    ones = jnp.ones((A_W, LANES), BF16)
    outs = []
    for g in range(A_GROUP):
        lanes = slice(g * LANES, (g + 1) * LANES)
        q = _head_norm(q_ref[:, lanes].astype(F32), bd, gain_ref[0:1, :]) * (LOG2E * HEAD_DIM ** -0.5)
        q2 = jnp.concatenate([q * lo, q * hi], axis=0).astype(BF16)
        s = _dot_nt(q2, kn_ref[pl.ds(w0, A_W), lanes]) + bias_ref[g]
        s = jnp.where(valid, s, NEG)
        p = jnp.exp2(s - jnp.max(s, axis=-1, keepdims=True)).astype(BF16)
        v_aug = jnp.concatenate([v_ref[pl.ds(w0, A_W), lanes], ones], axis=1)
        o2 = _dot(p, v_aug)
        o2 = o2[:, :LANES] * (1.0 / o2[:, LANES:])
        outs.append(jnp.where(first_head, o2[:A_TQ], o2[A_TQ:]).astype(o_ref.dtype))
    o_ref[...] = jnp.concatenate(outs, axis=1)


def _attn_a(qkv, bias, bd, gains, bsz, seq):
    rows = PAD + seq
    width = A_GROUP * LANES
    groups = N_PAIRS // A_GROUP
    return pl.pallas_call(
        _attn_a_body,
        grid=(bsz, groups, seq // A_TQ),
        in_specs=[
            pl.BlockSpec((None, A_TQ, width), lambda b, j, i: (b, i + PAD // A_TQ, j)),
            pl.BlockSpec((None, rows, width), lambda b, j, i: (b, 0, groups + j), pipeline_mode=pl.Buffered(1)),
            pl.BlockSpec((None, rows, width), lambda b, j, i: (b, 0, 2 * groups + j), pipeline_mode=pl.Buffered(1)),
            pl.BlockSpec((A_GROUP, 2 * A_TQ, A_W), lambda b, j, i: (j, 0, 0), pipeline_mode=pl.Buffered(1)),
            _const_spec((LANES, LANES)),
            _const_spec(gains.shape),
        ],
        out_specs=pl.BlockSpec((None, A_TQ, width), lambda b, j, i: (b, i, j)),
        out_shape=jax.ShapeDtypeStruct((bsz, seq, D_MODEL), BF16),
        scratch_shapes=[pltpu.VMEM((rows, width), BF16)],
        compiler_params=_cparams(("parallel", "parallel", "arbitrary")),
        name="attn_band",
    )(qkv, qkv, qkv, bias, bd, gains)


def _band_bias(rel_bias):
    r = jnp.arange(A_TQ)[:, None]
    c = jnp.arange(A_W)[None, :]
    in_band = (c // CHUNK >= r // CHUNK) & (c // CHUNK <= r // CHUNK + LEFT_CHUNKS)
    period = A_W + A_TQ
    m = jnp.arange(period)
    c_minus_r = jnp.where(m < A_W, m, m - period)
    idx = jnp.clip(PAD - c_minus_r, -MAX_REL, MAX_REL) + MAX_REL
    line = rel_bias.astype(F32)[:, idx] * LOG2E
    toep = jnp.tile(line, (1, A_TQ))[:, :A_TQ * (period - 1)].reshape(-1, A_TQ, period - 1)[:, :, :A_W]
    bias = jnp.where(in_band[None], toep, NEG)
    return bias.reshape(N_PAIRS, 2 * A_TQ, A_W)


def _softplus(z):
    return jnp.maximum(z, 0.0) + jnp.log(1.0 + jnp.exp2(jnp.abs(z) * (-LOG2E)))


def _attn_c_body(q_ref, k_ref, v_ref, tri_ref, o_ref, acc_ref):
    i = pl.program_id(2)
    tri = tri_ref[...]
    lo, hi = _lane_half_masks(BF16)
    q = q_ref[...]
    q_pairs = [jnp.concatenate([q[:, g * LANES:(g + 1) * LANES] * lo,
                                q[:, g * LANES:(g + 1) * LANES] * hi], axis=0) for g in range(C_GROUP)]
    row = lax.broadcasted_iota(jnp.int32, (2 * C_TQ, C_TK), 0)
    colk = lax.broadcasted_iota(jnp.int32, (2 * C_TQ, C_TK), 1)
    causal = colk < jnp.where(row >= C_TQ, row - C_TQ, row)
    first_head = lax.broadcasted_iota(jnp.int32, (C_TQ, LANES), 1) < HEAD_DIM

    def block(kb, runs, diag):
        r0 = pl.multiple_of(kb * C_TK, C_TK)
        pv = []
        new_runs = []
        for g, (q2, run) in enumerate(zip(q_pairs, runs)):
            lanes = slice(g * LANES, (g + 1) * LANES)
            z = _dot_nt(q2, k_ref[pl.ds(r0, C_TK), lanes])
            sp = _softplus(z)
            if diag:
                sp = jnp.where(causal, sp, 0.0)
            tail = _dot(sp.astype(BF16), tri)
            a = jnp.exp(z - (tail + run))
            if diag:
                a = jnp.where(causal, a, 0.0)
            c2 = _dot(a.astype(BF16), v_ref[pl.ds(r0, C_TK), lanes])
            pv.append(jnp.where(first_head, c2[:C_TQ], c2[C_TQ:]))
            new_runs.append(run + jnp.sum(sp, axis=-1, keepdims=True))
        return pv, tuple(new_runs)

    zero_run = jnp.zeros((2 * C_TQ, 1), F32)
    pv, runs = block(i, (zero_run,) * C_GROUP, True)
    acc_ref[...] = jnp.concatenate(pv, axis=1)

    def single(runs):
        pv, runs = block(i - 1, runs, False)
        acc_ref[...] += jnp.concatenate(pv, axis=1)
        return runs

    odd = i % 2
    runs = lax.cond(odd == 1, single, lambda r: r, runs)

    def step(n, runs):
        kb = i - 1 - odd - 2 * n
        pv_a, runs = block(kb, runs, False)
        pv_b, runs = block(kb - 1, runs, False)
        acc_ref[...] += jnp.concatenate(pv_a, axis=1) + jnp.concatenate(pv_b, axis=1)
        return runs

    lax.fori_loop(0, i // 2, step, runs)
    o_ref[...] = acc_ref[...].astype(o_ref.dtype)


def _attn_c(qkv, tri, bsz, seq):
    width = C_GROUP * LANES
    groups = N_PAIRS // C_GROUP
    return pl.pallas_call(
        _attn_c_body,
        grid=(bsz, groups, seq // C_TQ),
        in_specs=[
            pl.BlockSpec((None, C_TQ, width), lambda b, j, i: (b, i, j)),
            pl.BlockSpec((None, seq, width), lambda b, j, i: (b, 0, groups + j)),
            pl.BlockSpec((None, seq, width), lambda b, j, i: (b, 0, 2 * groups + j)),
            _const_spec((C_TK, C_TK)),
        ],
        out_specs=pl.BlockSpec((None, C_TQ, width), lambda b, j, i: (b, i, j)),
        out_shape=jax.ShapeDtypeStruct((bsz, seq, D_MODEL), BF16),
        scratch_shapes=[pltpu.VMEM((C_TQ, width), F32)],
        compiler_params=_cparams(("parallel", "parallel", "arbitrary")),
        name="attn_stick",
    )(qkv, qkv, qkv, tri)


def _attn_b_body(qt_ref, qi_ref, wi_ref, k_ref, vt_ref, ki_ref, o_ref,
                 key_ref, bias_ref, qm_ref, m_ref, l_ref, acc_ref, *, topk):
    i = pl.program_id(1)
    nkb = i + 1
    lane_q = lax.broadcasted_iota(jnp.int32, (1, B_TQ), 1)
    limit = ((i * B_TQ + lane_q) // CHUNK + 1) * CHUNK
    row_k = lax.broadcasted_iota(jnp.int32, (B_TK, B_TQ), 0)
    qi = qi_ref[...]
    wi = wi_ref[...] * (IDX_HEADS ** -0.5)

    def score_group(sb, carry):
        for u in range(B_SCAN // B_TK):
            r0 = pl.multiple_of(sb * B_SCAN + u * B_TK, B_TK)
            logits = _dot_nt(ki_ref[pl.ds(r0, B_TK), :], qi)
            score = jnp.zeros((B_TK, B_TQ), F32)
            for h in range(IDX_HEADS):
                score = score + wi[h:h + 1, :] * jnp.maximum(logits[:, h * B_TQ:(h + 1) * B_TQ], 0.0)
            bits = pltpu.bitcast(score, jnp.int32)
            okey = bits ^ ((bits >> 31) & jnp.int32(0x7FFFFFFF))
            adm = (row_k + r0) < limit
            key_ref[pl.ds(r0, B_TK), :] = jnp.where(adm, okey, jnp.int32(-(2 ** 31)))
        return carry

    n_scan = (i + B_SCAN // B_TK) // (B_SCAN // B_TK)
    lax.fori_loop(0, n_scan, score_group, 0)
    row_s = lax.broadcasted_iota(jnp.int32, (B_SCAN, B_TQ), 0)

    def count(pred_fn):
        def body(sb, c):
            r0 = pl.multiple_of(sb * B_SCAN, B_SCAN)
            hit = pred_fn(key_ref[pl.ds(r0, B_SCAN), :], row_s + sb * B_SCAN)
            return c + jnp.sum(jnp.where(hit, 1, 0).reshape(B_SCAN // 8, 8, B_TQ), axis=0)
        part = lax.fori_loop(0, n_scan, body, jnp.zeros((8, B_TQ), jnp.int32))
        return jnp.sum(part, axis=0, keepdims=True)

    sign = jnp.int32(-(2 ** 31))

    def bit_step(n, t):
        cand = t | (jnp.int32(1) << (31 - n))
        cnt = count(lambda kk, pos: kk >= (cand ^ sign))
        return jnp.where(cnt >= topk, cand, t)

    thr = lax.fori_loop(0, 32, bit_step, jnp.zeros((1, B_TQ), jnp.int32)) ^ sign
    n_gt = count(lambda kk, pos: kk > thr)
    n_ge = count(lambda kk, pos: kk >= thr)
    need = topk - n_gt

    def tie_search():
        def idx_step(n, j):
            cand = j | (jnp.int32(1) << (15 - n))
            cnt = count(lambda kk, pos: (kk == thr) & (pos < cand))
            return jnp.where(cnt < need, cand, j)
        return lax.fori_loop(0, 16, idx_step, jnp.zeros((1, B_TQ), jnp.int32))

    tie_hi = lax.cond(jnp.max(n_ge) > topk, tie_search,
                      lambda: jnp.full((1, B_TQ), 2 ** 30, jnp.int32))

    def bias_block(kb, carry):
        r0 = pl.multiple_of(kb * B_TK, B_TK)
        kk = key_ref[pl.ds(r0, B_TK), :]
        pos = row_k + kb * B_TK
        sel = (kk > thr) | ((kk == thr) & (pos <= tie_hi))
        sel = sel & (pos < limit)
        bias_ref[pl.ds(r0, B_TK), :] = jnp.where(sel, 0.0, NEG)
        return carry

    n_att = (nkb + B_UNROLL - 1) // B_UNROLL
    lax.fori_loop(0, n_att * B_UNROLL, bias_block, 0)

    sub = lax.broadcasted_iota(jnp.int32, (LANES, 1), 0)
    for j in range(N_PAIRS):
        qt = qt_ref[j * LANES:(j + 1) * LANES, :]
        for half in range(2):
            qm_ref[2 * j + half] = jnp.where((sub < HEAD_DIM) == (half == 0), qt, jnp.zeros_like(qt))
    m_ref[...] = jnp.full(m_ref.shape, NEG, F32)
    l_ref[...] = jnp.zeros(l_ref.shape, F32)
    acc_ref[...] = jnp.zeros(acc_ref.shape, F32)

    def kv_step(n, carry):
        starts = [pl.multiple_of((n * B_UNROLL + u) * B_TK, B_TK) for u in range(B_UNROLL)]
        biases = [bias_ref[pl.ds(r0, B_TK), :] for r0 in starts]
        for j in range(N_PAIRS):
            k_blks = [k_ref[pl.ds(r0, B_TK), j * LANES:(j + 1) * LANES] for r0 in starts]
            for half in range(2):
                h = 2 * j + half
                d0 = h * HEAD_DIM
                ss = [_dot(k_blk, qm_ref[h]) + bias for k_blk, bias in zip(k_blks, biases)]
                m_old = m_ref[h:h + 1, :]
                m_new = m_old
                for s in ss:
                    m_new = jnp.maximum(m_new, jnp.max(s, axis=0, keepdims=True))
                alpha = jnp.exp2(m_old - m_new)
                ps = [jnp.exp2(s - m_new) for s in ss]
                l_new = alpha * l_ref[h:h + 1, :]
                pv = alpha * acc_ref[d0:d0 + HEAD_DIM, :]
                for r0, p in zip(starts, ps):
                    l_new = l_new + jnp.sum(p, axis=0, keepdims=True)
                    pv = pv + _dot(vt_ref[d0:d0 + HEAD_DIM, pl.ds(r0, B_TK)], p.astype(BF16))
                l_ref[h:h + 1, :] = l_new
                m_ref[h:h + 1, :] = m_new
                acc_ref[d0:d0 + HEAD_DIM, :] = pv
        return carry

    lax.fori_loop(0, n_att, kv_step, 0)
    for h in range(N_HEADS):
        d0 = h * HEAD_DIM
        o_ref[d0:d0 + HEAD_DIM, :] = (acc_ref[d0:d0 + HEAD_DIM, :]
                                      * (1.0 / l_ref[h:h + 1, :])).astype(o_ref.dtype)


def _attn_b(qt, qi, wi, k, vt, ki, bsz, seq, topk):
    nblk = seq // B_TQ
    one = pl.Buffered(1)
    return pl.pallas_call(
        functools.partial(_attn_b_body, topk=topk),
        grid=(bsz, nblk),
        in_specs=[
            pl.BlockSpec((None, None, D_MODEL, B_TQ), lambda b, i: (b, i, 0, 0)),
            pl.BlockSpec((None, None, IDX_HEADS * B_TQ, IDX_DIM), lambda b, i: (b, i, 0, 0)),
            pl.BlockSpec((None, None, IDX_HEADS, B_TQ), lambda b, i: (b, i, 0, 0)),
            pl.BlockSpec((None, seq, D_MODEL), lambda b, i: (b, 0, 0), pipeline_mode=one),
            pl.BlockSpec((None, D_MODEL, seq), lambda b, i: (b, 0, 0), pipeline_mode=one),
            pl.BlockSpec((None, seq, IDX_DIM), lambda b, i: (b, 0, 0), pipeline_mode=one),
        ],
        out_specs=pl.BlockSpec((None, None, D_MODEL, B_TQ), lambda b, i: (b, i, 0, 0)),
        out_shape=jax.ShapeDtypeStruct((bsz, nblk, D_MODEL, B_TQ), BF16),
        scratch_shapes=[
            pltpu.VMEM((seq, B_TQ), jnp.int32),
            pltpu.VMEM((seq, B_TQ), F32),
            pltpu.VMEM((N_HEADS, LANES, B_TQ), BF16),
            pltpu.VMEM((N_HEADS, B_TQ), F32),
            pltpu.VMEM((N_HEADS, B_TQ), F32),
            pltpu.VMEM((D_MODEL, B_TQ), F32),
        ],
        compiler_params=_cparams(("parallel", "arbitrary")),
        name="attn_topk",
    )(qt, qi, wi, k, vt, ki)


def _mlp_body(x_ref, o_ref, wo_ref, g_ref, win_ref, cw_ref, cb_ref, wd_ref, y_ref, carry_ref, *, blocks_per_seq):
    i = pl.program_id(0)

    @pl.when((i % blocks_per_seq) == 0)
    def _():
        carry_ref[...] = jnp.zeros_like(carry_ref)

    x1 = x_ref[...] + _dot(o_ref[...], wo_ref[...])
    y_ref[...] = x1
    hb = _rms(x1, g_ref[...]).astype(BF16)
    def shifted(a, prev, n):
        row = lax.broadcasted_iota(jnp.int32, prev.shape, 0)
        r = pltpu.roll(a, n, 0)
        top = jnp.where(row < n, pltpu.roll(prev, n, 0), r[:8])
        return jnp.concatenate([top, r[8:]], axis=0)

    for f0 in range(0, D_FF, F_TILE):
        fw = min(F_TILE, D_FF - f0)
        branches = []
        for c0 in (f0, D_FF + f0):
            a = _dot(hb, win_ref[:, c0:c0 + fw])
            prev = carry_ref[:, c0:c0 + fw]
            carry_ref[:, c0:c0 + fw] = a[MLP_TM - 8:, :]
            branches.append(cb_ref[:, c0:c0 + fw]
                            + shifted(a, prev, 2) * cw_ref[0:1, c0:c0 + fw]
                            + shifted(a, prev, 1) * cw_ref[1:2, c0:c0 + fw]
                            + a * cw_ref[2:3, c0:c0 + fw])
        gate, up = branches
        act = (gate * (1.0 / (1.0 + jnp.exp(-gate))) * up).astype(BF16)
        y_ref[...] += _dot(act, wd_ref[f0:f0 + fw, :])


def _mlp(x, o, wo, g, w_in, conv_w, conv_b, w_down, seq):
    m = x.shape[0]
    return pl.pallas_call(
        functools.partial(_mlp_body, blocks_per_seq=seq // MLP_TM),
        grid=(m // MLP_TM,),
        in_specs=[
            pl.BlockSpec((MLP_TM, D_MODEL), lambda i: (i, 0)),
            pl.BlockSpec((MLP_TM, D_MODEL), lambda i: (i, 0)),
            _const_spec((D_MODEL, D_MODEL)),
            _const_spec((1, D_MODEL)),
            _const_spec((D_MODEL, 2 * D_FF)),
            _const_spec((CONV_W, 2 * D_FF)),
            _const_spec((1, 2 * D_FF)),
            _const_spec((D_FF, D_MODEL)),
        ],
        out_specs=pl.BlockSpec((MLP_TM, D_MODEL), lambda i: (i, 0)),
        out_shape=jax.ShapeDtypeStruct((m, D_MODEL), F32),
        scratch_shapes=[pltpu.VMEM((8, 2 * D_FF), F32)],
        compiler_params=_cparams(("arbitrary",)),
        name="oproj_conv_mlp",
    )(x, o, wo, g.reshape(1, D_MODEL), w_in, conv_w, conv_b.reshape(1, 2 * D_FF), w_down)


def _rope_tables(seq):
    inv = ROPE_THETA ** (-jnp.arange(0, HEAD_DIM, 2, dtype=F32) / HEAD_DIM)
    ang = jnp.arange(seq, dtype=F32)[:, None] * inv[None, :]
    cos, sin = jnp.cos(ang), jnp.sin(ang)
    return jnp.tile(cos, (1, 4)), jnp.tile(jnp.concatenate([-sin, sin], axis=1), (1, 2))


def _pair_gain(gq, gk):
    return jnp.stack([jnp.tile(gq.astype(F32), 2), jnp.tile(gk.astype(F32), 2)])


def kernel(x, norm1_g, norm2_g, a_w_qkv, a_q_norm, a_k_norm, a_rel_bias, a_w_o,
           b_w_in, b_q_norm, b_k_norm, b_w_o, c_w_qkv, c_w_o,
           ffn_w_in, ffn_conv_w, ffn_conv_b, ffn_w_down):
    bsz, seq, d_model = x.shape
    assert d_model == D_MODEL and x.dtype == F32
    assert all(seq % blk == 0 for blk in (TM, MLP_TM, A_TQ, B_SCAN, C_TQ)) and PAD % TM == 0
    depth = norm1_g.shape[0]
    rows = bsz * seq
    blk = jnp.arange(LANES) // HEAD_DIM
    bd = (blk[:, None] == blk[None, :]).astype(BF16)
    tri = (jnp.arange(C_TK)[:, None] >= jnp.arange(C_TK)[None, :]).astype(BF16)
    rope_tabs = _rope_tables(seq)
    no_gain = jnp.ones((2, LANES), F32)
    plain = (None, None, 1.0)
    q_scale = HEAD_DIM ** -0.5

    ia = ib = ic = 0
    for layer in range(depth):
        kind = layer % 3
        if kind == 0:
            gains = _pair_gain(a_q_norm[ia], a_k_norm[ia])
            qkv = _project(x, norm1_g[layer], a_w_qkv[ia].astype(BF16), [plain] * (3 * N_PAIRS),
                           no_gain, bd, lead_rows=PAD)
            o = _attn_a(qkv, _band_bias(a_rel_bias[ia]), bd, gains, bsz, seq)
            w_o = a_w_o[ia]
            ia += 1
        elif kind == 1:
            gains = _pair_gain(b_q_norm[ib], b_k_norm[ib])
            n_in = b_w_in.shape[-1]
            n_main = 3 * D_MODEL + IDX_HEADS * IDX_DIM
            w = jnp.pad(b_w_in[ib], ((0, 0), (0, n_main + LANES - n_in))).astype(BF16)
            modes = ([(0, "all", q_scale * LOG2E)] * N_PAIRS + [(1, "all", 1.0)] * N_PAIRS + [plain] * N_PAIRS
                     + [(None, "all", IDX_DIM ** -0.5)] * (IDX_HEADS * IDX_DIM // LANES)
                     + [(None, "low_head", 1.0)])
            proj, tail = _project(x, norm1_g[layer], w, modes, gains, bd, rope_tabs=rope_tabs, aux=True)
            nblk = seq // B_TQ
            q = proj[..., :D_MODEL]
            k = proj[..., D_MODEL:2 * D_MODEL]
            v = proj[..., 2 * D_MODEL:3 * D_MODEL]
            qi = proj[..., 3 * D_MODEL:n_main]
            ki = proj[..., n_main:n_main + IDX_DIM]
            wi = tail[..., IDX_DIM:IDX_DIM + IDX_HEADS]
            qt = q.reshape(bsz, nblk, B_TQ, D_MODEL).swapaxes(2, 3)
            vt = v.swapaxes(1, 2)
            qi_t = (qi.reshape(bsz, nblk, B_TQ, IDX_HEADS, IDX_DIM).swapaxes(2, 3)
                    .reshape(bsz, nblk, IDX_HEADS * B_TQ, IDX_DIM))
            wi_t = wi.reshape(bsz, nblk, B_TQ, IDX_HEADS).swapaxes(2, 3)
            ot = _attn_b(qt, qi_t, wi_t, k, vt, ki, bsz, seq, min(TOPK_MAX, seq // 4))
            o = ot.swapaxes(2, 3).reshape(bsz, seq, D_MODEL)
            w_o = b_w_o[ib]
            ib += 1
        else:
            modes = [(None, None, q_scale)] * N_PAIRS + [plain] * (2 * N_PAIRS)
            qkv = _project(x, norm1_g[layer], c_w_qkv[ic].astype(BF16), modes, no_gain, bd)
            o = _attn_c(qkv, tri, bsz, seq)
            w_o = c_w_o[ic]
            ic += 1
        x = _mlp(x.reshape(rows, D_MODEL), o.reshape(rows, D_MODEL), w_o.astype(BF16), norm2_g[layer],
                 ffn_w_in[layer].astype(BF16), ffn_conv_w[layer], ffn_conv_b[layer],
                 ffn_w_down[layer].astype(BF16), seq).reshape(bsz, seq, D_MODEL)
    return x
```

```python
import functools

import jax
import jax.numpy as jnp
from jax import lax
from jax.experimental import pallas as pl
from jax.experimental.pallas import tpu as pltpu

D_MODEL = 1024
N_HEADS = 16
HEAD_DIM = 64
N_PAIRS = N_HEADS // 2
LANES = 128
CHUNK = 64
LEFT_CHUNKS = 8
PAD = LEFT_CHUNKS * CHUNK
MAX_REL = 256
IDX_HEADS = 8
IDX_DIM = 64
TOPK_MAX = 256
D_FF = 2816
CONV_W = 3
ROPE_THETA = 10000.0
EPS = 1e-6
NEG = -1e30

TM = 512
A_TQ = 128
A_W = A_TQ + PAD
A_GROUP = 8
C_GROUP = 4
LOG2E = 1.4426950408889634
B_TQ = 128
B_TK = 128
B_SCAN = 512
B_UNROLL = 4
MLP_TM = 1024
C_TQ = 256
C_TK = 256
F_TILE = 1408
VMEM_LIMIT = 56 * 1024 * 1024

BF16 = jnp.bfloat16
F32 = jnp.float32


def _cparams(sem):
    return pltpu.CompilerParams(dimension_semantics=sem, vmem_limit_bytes=VMEM_LIMIT)


def _const_spec(shape):
    nd = len(shape)
    return pl.BlockSpec(shape, lambda *_: (0,) * nd, pipeline_mode=pl.Buffered(1))


def _dot(a, b):
    return jnp.dot(a, b, preferred_element_type=F32)


def _dot_nt(a, b):
    return lax.dot_general(a, b, (((1,), (1,)), ((), ())), preferred_element_type=F32)


def _rms(x, g):
    return x * lax.rsqrt(jnp.mean(x * x, axis=-1, keepdims=True) + EPS) * g


def _lane_half_masks(dtype):
    lane = lax.broadcasted_iota(jnp.int32, (1, LANES), 1)
    lo = (lane < HEAD_DIM).astype(dtype)
    return lo, (1 - lo).astype(dtype)


def _head_norm(y, bd, gain):
    ss = _dot((y * y).astype(BF16), bd)
    return y * lax.rsqrt(ss * (1.0 / HEAD_DIM) + EPS) * gain


def _rope(y, cos_t, sin_t):
    lane = lax.broadcasted_iota(jnp.int32, y.shape, 1)
    first = (lane % HEAD_DIM) < (HEAD_DIM // 2)
    partner = jnp.where(first, pltpu.roll(y, LANES - HEAD_DIM // 2, 1), pltpu.roll(y, HEAD_DIM // 2, 1))
    return y * cos_t + partner * sin_t


def _proj_body(*refs, modes, splits, use_rope, n_aux, lead):
    it = iter(refs)
    x_ref, g_ref, w_ref, bd_ref, gain_ref = next(it), next(it), next(it), next(it), next(it)
    cos_ref = sin_ref = None
    if use_rope:
        cos_ref, sin_ref = next(it), next(it)
    o_refs = [next(it) for _ in splits]
    aux_ref = next(it) if n_aux else None
    n_out = 128 * len(modes)
    dest = [(k, t) for k, n_tiles in enumerate(splits) for t in range(n_tiles)]

    def compute():
        hb = _rms(x_ref[...], g_ref[...]).astype(BF16)
        bd = bd_ref[...]
        for c0 in range(0, n_out, 512):
            wd = min(512, n_out - c0)
            y = _dot(hb, w_ref[:, c0:c0 + wd])
            for t0 in range(0, wd, LANES):
                norm_idx, rope, scale = modes[(c0 + t0) // LANES]
                yt = y[:, t0:t0 + LANES]
                if norm_idx is not None:
                    yt = _head_norm(yt, bd, gain_ref[norm_idx:norm_idx + 1, :])
                if rope == "all":
                    yt = _rope(yt, cos_ref[...], sin_ref[...])
                elif rope == "low_head":
                    lane = lax.broadcasted_iota(jnp.int32, yt.shape, 1)
                    yt = jnp.where(lane < HEAD_DIM, _rope(yt, cos_ref[...], sin_ref[...]), yt)
                if scale != 1.0:
                    yt = yt * scale
                k, t = dest[(c0 + t0) // LANES]
                o_refs[k][:, t * LANES:(t + 1) * LANES] = yt.astype(o_refs[k].dtype)
                if n_aux and (c0 + t0) // LANES == len(modes) - 1:
                    aux_ref[...] = yt

    if lead:
        @pl.when(pl.program_id(1) < lead)
        def _():
            for o_ref in o_refs:
                o_ref[...] = jnp.zeros_like(o_ref)

        pl.when(pl.program_id(1) >= lead)(compute)
    else:
        compute()


def _project(x, g, w, modes, gains, bd, rope_tabs=None, aux=False, lead_rows=0, splits=None):
    bsz, seq, _ = x.shape
    n_out = LANES * len(modes)
    splits = tuple(splits) if splits is not None else (len(modes),)
    assert sum(splits) == len(modes)
    lead = lead_rows // TM
    nblk = seq // TM
    in_specs = [
        pl.BlockSpec((None, TM, D_MODEL), lambda b, i: (b, jnp.maximum(i - lead, 0), 0)),
        _const_spec((1, D_MODEL)),
        _const_spec((D_MODEL, n_out)),
        _const_spec((LANES, LANES)),
        _const_spec(gains.shape),
    ]
    args = [x, g.reshape(1, D_MODEL), w, bd, gains]
    if rope_tabs is not None:
        in_specs += [pl.BlockSpec((TM, LANES), lambda b, i: (i, 0))] * 2
        args += list(rope_tabs)
    out_shape = [jax.ShapeDtypeStruct((bsz, lead_rows + seq, n * LANES), BF16) for n in splits]
    out_specs = [pl.BlockSpec((None, TM, n * LANES), lambda b, i: (b, i, 0)) for n in splits]
    if aux:
        out_shape.append(jax.ShapeDtypeStruct((bsz, seq, LANES), F32))
        out_specs.append(pl.BlockSpec((None, TM, LANES), lambda b, i: (b, i, 0)))
    res = pl.pallas_call(
        functools.partial(_proj_body, modes=tuple(modes), splits=splits, use_rope=rope_tabs is not None,
                          n_aux=int(aux), lead=lead),
        grid=(bsz, lead + nblk),
        in_specs=in_specs,
        out_specs=out_specs,
        out_shape=out_shape,
        compiler_params=_cparams(("parallel", "arbitrary")),
        name="norm_proj",
    )(*args)
    return res if (aux or len(splits) > 1) else res[0]


def _attn_a_body(q_ref, k_ref, v_ref, bias_ref, bd_ref, gain_ref, o_ref, kn_ref):
    i = pl.program_id(2)
    bd = bd_ref[...]
    n_rows = kn_ref.shape[0]

    @pl.when(i == 0)
    def _():
        def norm_rows(c, carry):
            r0 = pl.multiple_of(c * TM, TM)
            for g in range(A_GROUP):
                kk = k_ref[pl.ds(r0, TM), g * LANES:(g + 1) * LANES].astype(F32)
                kn_ref[pl.ds(r0, TM), g * LANES:(g + 1) * LANES] = (
                    _head_norm(kk, bd, gain_ref[1:2, :]).astype(BF16))
            return carry
        lax.fori_loop(0, n_rows // TM, norm_rows, 0)

    lo, hi = _lane_half_masks(F32)
    w0 = pl.multiple_of(i * A_TQ, A_TQ)
    first_head = lax.broadcasted_iota(jnp.int32, (A_TQ, LANES), 1) < HEAD_DIM
    ones = jnp.ones((A_W, LANES), BF16)

    col = lax.broadcasted_iota(jnp.int32, (1, A_W), 1)
    valid = (col + (i * A_TQ - PAD)) >= 0
    outs = []
    for g in range(A_GROUP):
        lanes = slice(g * LANES, (g + 1) * LANES)
        q = _head_norm(q_ref[:, lanes].astype(F32), bd, gain_ref[0:1, :]) * (LOG2E * HEAD_DIM ** -0.5)
        q2 = jnp.concatenate([q * lo, q * hi], axis=0).astype(BF16)
        s = _dot_nt(q2, kn_ref[pl.ds(w0, A_W), lanes]) + bias_ref[g]
        s = jnp.where(valid, s, NEG)
        p = jnp.exp2(s - jnp.max(s, axis=-1, keepdims=True)).astype(BF16)
        v_aug = jnp.concatenate([v_ref[pl.ds(w0, A_W), lanes], ones], axis=1)
        o2 = _dot(p, v_aug)
        o2 = o2[:, :LANES] * (1.0 / o2[:, LANES:])
        outs.append(jnp.where(first_head, o2[:A_TQ], o2[A_TQ:]).astype(o_ref.dtype))
    o_ref[...] = jnp.concatenate(outs, axis=1)


def _attn_a(qkv, bias, bd, gains, bsz, seq):
    rows = PAD + seq
    width = A_GROUP * LANES
    groups = N_PAIRS // A_GROUP
    return pl.pallas_call(
        _attn_a_body,
        grid=(bsz, groups, seq // A_TQ),
        in_specs=[
            pl.BlockSpec((None, A_TQ, width), lambda b, j, i: (b, i + PAD // A_TQ, j)),
            pl.BlockSpec((None, rows, width), lambda b, j, i: (b, 0, groups + j), pipeline_mode=pl.Buffered(1)),
            pl.BlockSpec((None, rows, width), lambda b, j, i: (b, 0, 2 * groups + j), pipeline_mode=pl.Buffered(1)),
            pl.BlockSpec((A_GROUP, 2 * A_TQ, A_W), lambda b, j, i: (j, 0, 0), pipeline_mode=pl.Buffered(1)),
            _const_spec((LANES, LANES)),
            _const_spec(gains.shape),
        ],
        out_specs=pl.BlockSpec((None, A_TQ, width), lambda b, j, i: (b, i, j)),
        out_shape=jax.ShapeDtypeStruct((bsz, seq, D_MODEL), BF16),
        scratch_shapes=[pltpu.VMEM((rows, width), BF16)],
        compiler_params=_cparams(("parallel", "parallel", "arbitrary")),
        name="attn_band",
    )(qkv, qkv, qkv, bias, bd, gains)


def _band_bias(rel_bias):
    r = jnp.arange(A_TQ)[:, None]
    c = jnp.arange(A_W)[None, :]
    in_band = (c // CHUNK >= r // CHUNK) & (c // CHUNK <= r // CHUNK + LEFT_CHUNKS)
    period = A_W + A_TQ
    m = jnp.arange(period)
    c_minus_r = jnp.where(m < A_W, m, m - period)
    idx = jnp.clip(PAD - c_minus_r, -MAX_REL, MAX_REL) + MAX_REL
    line = rel_bias.astype(F32)[:, idx] * LOG2E
    toep = jnp.tile(line, (1, A_TQ))[:, :A_TQ * (period - 1)].reshape(-1, A_TQ, period - 1)[:, :, :A_W]
    bias = jnp.where(in_band[None], toep, NEG)
    return bias.reshape(N_PAIRS, 2 * A_TQ, A_W)


def _softplus(z):
    return jnp.maximum(z, 0.0) + jnp.log(1.0 + jnp.exp2(jnp.abs(z) * (-LOG2E)))


def _attn_c_body(q_ref, k_ref, v_ref, tri_ref, o_ref, acc_ref):
    i = pl.program_id(2)
    tri = tri_ref[...]
    lo, hi = _lane_half_masks(BF16)
    q = q_ref[...]
    q_pairs = [jnp.concatenate([q[:, g * LANES:(g + 1) * LANES] * lo,
                                q[:, g * LANES:(g + 1) * LANES] * hi], axis=0) for g in range(C_GROUP)]
    row = lax.broadcasted_iota(jnp.int32, (2 * C_TQ, C_TK), 0)
    colk = lax.broadcasted_iota(jnp.int32, (2 * C_TQ, C_TK), 1)
    causal = colk < jnp.where(row >= C_TQ, row - C_TQ, row)
    first_head = lax.broadcasted_iota(jnp.int32, (C_TQ, LANES), 1) < HEAD_DIM

    def block(kb, runs, diag):
        r0 = pl.multiple_of(kb * C_TK, C_TK)
        pv = []
        new_runs = []
        for g, (q2, run) in enumerate(zip(q_pairs, runs)):
            lanes = slice(g * LANES, (g + 1) * LANES)
            z = _dot_nt(q2, k_ref[pl.ds(r0, C_TK), lanes])
            sp = _softplus(z)
            if diag:
                sp = jnp.where(causal, sp, 0.0)
            tail = _dot(sp.astype(BF16), tri)
            a = jnp.exp(z - (tail + run))
            if diag:
                a = jnp.where(causal, a, 0.0)
            c2 = _dot(a.astype(BF16), v_ref[pl.ds(r0, C_TK), lanes])
            pv.append(jnp.where(first_head, c2[:C_TQ], c2[C_TQ:]))
            new_runs.append(run + jnp.sum(sp, axis=-1, keepdims=True))
        return pv, tuple(new_runs)

    zero_run = jnp.zeros((2 * C_TQ, 1), F32)
    pv, runs = block(i, (zero_run,) * C_GROUP, True)
    acc_ref[...] = jnp.concatenate(pv, axis=1)

    def single(runs):
        pv, runs = block(i - 1, runs, False)
        acc_ref[...] += jnp.concatenate(pv, axis=1)
        return runs

    odd = i % 2
    runs = lax.cond(odd == 1, single, lambda r: r, runs)

    def step(n, runs):
        kb = i - 1 - odd - 2 * n
        pv_a, runs = block(kb, runs, False)
        pv_b, runs = block(kb - 1, runs, False)
        acc_ref[...] += jnp.concatenate(pv_a, axis=1) + jnp.concatenate(pv_b, axis=1)
        return runs

    lax.fori_loop(0, i // 2, step, runs)
    o_ref[...] = acc_ref[...].astype(o_ref.dtype)


def _attn_c(qkv, tri, bsz, seq):
    width = C_GROUP * LANES
    groups = N_PAIRS // C_GROUP
    return pl.pallas_call(
        _attn_c_body,
        grid=(bsz, groups, seq // C_TQ),
        in_specs=[
            pl.BlockSpec((None, C_TQ, width), lambda b, j, i: (b, i, j)),
            pl.BlockSpec((None, seq, width), lambda b, j, i: (b, 0, groups + j)),
            pl.BlockSpec((None, seq, width), lambda b, j, i: (b, 0, 2 * groups + j)),
            _const_spec((C_TK, C_TK)),
        ],
        out_specs=pl.BlockSpec((None, C_TQ, width), lambda b, j, i: (b, i, j)),
        out_shape=jax.ShapeDtypeStruct((bsz, seq, D_MODEL), BF16),
        scratch_shapes=[pltpu.VMEM((C_TQ, width), F32)],
        compiler_params=_cparams(("parallel", "parallel", "arbitrary")),
        name="attn_stick",
    )(qkv, qkv, qkv, tri)


def _attn_b_body(q_ref, qi_ref, wi_ref, k_ref, vt_ref, ki_ref, o_ref,
                 key_ref, bias_ref, qm_ref, m_ref, l_ref, acc_ref, *, topk):
    i = pl.program_id(1)
    nkb = i + 1
    lane_q = lax.broadcasted_iota(jnp.int32, (1, B_TQ), 1)
    limit = ((i * B_TQ + lane_q) // CHUNK + 1) * CHUNK
    row_k = lax.broadcasted_iota(jnp.int32, (B_TK, B_TQ), 0)
    qi = qi_ref[...]
    wi = wi_ref[...] * (IDX_HEADS ** -0.5)

    def score_group(sb, carry):
        for u in range(B_SCAN // B_TK):
            r0 = pl.multiple_of(sb * B_SCAN + u * B_TK, B_TK)
            logits = _dot_nt(ki_ref[pl.ds(r0, B_TK), :], qi)
            score = jnp.zeros((B_TK, B_TQ), F32)
            for h in range(IDX_HEADS):
                score = score + wi[h:h + 1, :] * jnp.maximum(logits[:, h * B_TQ:(h + 1) * B_TQ], 0.0)
            bits = pltpu.bitcast(score, jnp.int32)
            okey = bits ^ ((bits >> 31) & jnp.int32(0x7FFFFFFF))
            adm = (row_k + r0) < limit
            key_ref[pl.ds(r0, B_TK), :] = jnp.where(adm, okey, jnp.int32(-(2 ** 31)))
        return carry

    n_scan = (i + B_SCAN // B_TK) // (B_SCAN // B_TK)
    lax.fori_loop(0, n_scan, score_group, 0)
    row_s = lax.broadcasted_iota(jnp.int32, (B_SCAN, B_TQ), 0)

    def count(pred_fn):
        def body(sb, c):
            r0 = pl.multiple_of(sb * B_SCAN, B_SCAN)
            hit = pred_fn(key_ref[pl.ds(r0, B_SCAN), :], row_s + sb * B_SCAN)
            return c + jnp.sum(jnp.where(hit, 1, 0).reshape(B_SCAN // 8, 8, B_TQ), axis=0)
        part = lax.fori_loop(0, n_scan, body, jnp.zeros((8, B_TQ), jnp.int32))
        return jnp.sum(part, axis=0, keepdims=True)

    sign = jnp.int32(-(2 ** 31))

    def bit_step(n, t):
        cand = t | (jnp.int32(1) << (31 - n))
        cnt = count(lambda kk, pos: kk >= (cand ^ sign))
        return jnp.where(cnt >= topk, cand, t)

    thr = lax.fori_loop(0, 32, bit_step, jnp.zeros((1, B_TQ), jnp.int32)) ^ sign
    n_gt = count(lambda kk, pos: kk > thr)
    n_ge = count(lambda kk, pos: kk >= thr)
    need = topk - n_gt

    def tie_search():
        def idx_step(n, j):
            cand = j | (jnp.int32(1) << (15 - n))
            cnt = count(lambda kk, pos: (kk == thr) & (pos < cand))
            return jnp.where(cnt < need, cand, j)
        return lax.fori_loop(0, 16, idx_step, jnp.zeros((1, B_TQ), jnp.int32))

    tie_hi = lax.cond(jnp.max(n_ge) > topk, tie_search,
                      lambda: jnp.full((1, B_TQ), 2 ** 30, jnp.int32))

    def bias_block(kb, carry):
        r0 = pl.multiple_of(kb * B_TK, B_TK)
        kk = key_ref[pl.ds(r0, B_TK), :]
        pos = row_k + kb * B_TK
        sel = (kk > thr) | ((kk == thr) & (pos <= tie_hi))
        sel = sel & (pos < limit)
        bias_ref[pl.ds(r0, B_TK), :] = jnp.where(sel, 0.0, NEG)
        return carry

    n_att = (nkb + B_UNROLL - 1) // B_UNROLL
    lax.fori_loop(0, n_att * B_UNROLL, bias_block, 0)

    sub = lax.broadcasted_iota(jnp.int32, (LANES, 1), 0)
    for j in range(N_PAIRS):
        qt = q_ref[:, j * LANES:(j + 1) * LANES].astype(F32).T.astype(BF16)
        for half in range(2):
            qm_ref[2 * j + half] = jnp.where((sub < HEAD_DIM) == (half == 0), qt, jnp.zeros_like(qt))
    m_ref[...] = jnp.full(m_ref.shape, NEG, F32)
    l_ref[...] = jnp.zeros(l_ref.shape, F32)
    acc_ref[...] = jnp.zeros(acc_ref.shape, F32)

    def kv_step(n, carry):
        starts = [pl.multiple_of((n * B_UNROLL + u) * B_TK, B_TK) for u in range(B_UNROLL)]
        biases = [bias_ref[pl.ds(r0, B_TK), :] for r0 in starts]
        for j in range(N_PAIRS):
            k_blks = [k_ref[pl.ds(r0, B_TK), j * LANES:(j + 1) * LANES] for r0 in starts]
            for half in range(2):
                h = 2 * j + half
                d0 = h * HEAD_DIM
                ss = [_dot(k_blk, qm_ref[h]) + bias for k_blk, bias in zip(k_blks, biases)]
                m_old = m_ref[h:h + 1, :]
                m_new = m_old
                for s in ss:
                    m_new = jnp.maximum(m_new, jnp.max(s, axis=0, keepdims=True))
                alpha = jnp.exp2(m_old - m_new)
                ps = [jnp.exp2(s - m_new) for s in ss]
                l_new = alpha * l_ref[h:h + 1, :]
                pv = alpha * acc_ref[d0:d0 + HEAD_DIM, :]
                for r0, p in zip(starts, ps):
                    l_new = l_new + jnp.sum(p, axis=0, keepdims=True)
                    pv = pv + _dot(vt_ref[d0:d0 + HEAD_DIM, pl.ds(r0, B_TK)], p.astype(BF16))
                l_ref[h:h + 1, :] = l_new
                m_ref[h:h + 1, :] = m_new
                acc_ref[d0:d0 + HEAD_DIM, :] = pv
        return carry

    lax.fori_loop(0, n_att, kv_step, 0)
    for j in range(N_PAIRS):
        d0 = j * LANES
        pair = jnp.concatenate(
            [acc_ref[d0 + u * HEAD_DIM:d0 + (u + 1) * HEAD_DIM, :] * (1.0 / l_ref[2 * j + u:2 * j + u + 1, :])
             for u in range(2)], axis=0)
        o_ref[:, d0:d0 + LANES] = pair.T.astype(o_ref.dtype)


def _attn_b(q, qi, wi, k, vt, ki, bsz, seq, topk):
    nblk = seq // B_TQ
    one = pl.Buffered(1)
    return pl.pallas_call(
        functools.partial(_attn_b_body, topk=topk),
        grid=(bsz, nblk),
        in_specs=[
            pl.BlockSpec((None, B_TQ, D_MODEL), lambda b, i: (b, i, 0)),
            pl.BlockSpec((None, None, IDX_HEADS * B_TQ, IDX_DIM), lambda b, i: (b, i, 0, 0)),
            pl.BlockSpec((None, None, IDX_HEADS, B_TQ), lambda b, i: (b, i, 0, 0)),
            pl.BlockSpec((None, seq, D_MODEL), lambda b, i: (b, 0, 0), pipeline_mode=one),
            pl.BlockSpec((None, D_MODEL, seq), lambda b, i: (b, 0, 0), pipeline_mode=one),
            pl.BlockSpec((None, seq, IDX_DIM), lambda b, i: (b, 0, 0), pipeline_mode=one),
        ],
        out_specs=pl.BlockSpec((None, B_TQ, D_MODEL), lambda b, i: (b, i, 0)),
        out_shape=jax.ShapeDtypeStruct((bsz, seq, D_MODEL), BF16),
        scratch_shapes=[
            pltpu.VMEM((seq, B_TQ), jnp.int32),
            pltpu.VMEM((seq, B_TQ), F32),
            pltpu.VMEM((N_HEADS, LANES, B_TQ), BF16),
            pltpu.VMEM((N_HEADS, B_TQ), F32),
            pltpu.VMEM((N_HEADS, B_TQ), F32),
            pltpu.VMEM((D_MODEL, B_TQ), F32),
        ],
        compiler_params=_cparams(("parallel", "arbitrary")),
        name="attn_topk",
    )(q, qi, wi, k, vt, ki)


def _mlp_body(x_ref, o_ref, wo_ref, g_ref, win_ref, cw_ref, cb_ref, wd_ref, y_ref, carry_ref, *, blocks_per_seq):
    i = pl.program_id(0)

    @pl.when((i % blocks_per_seq) == 0)
    def _():
        carry_ref[...] = jnp.zeros_like(carry_ref)

    x1 = x_ref[...] + _dot(o_ref[...], wo_ref[...])
    y_ref[...] = x1
    hb = _rms(x1, g_ref[...]).astype(BF16)
    def shifted(a, prev, n):
        row = lax.broadcasted_iota(jnp.int32, prev.shape, 0)
        r = pltpu.roll(a, n, 0)
        top = jnp.where(row < n, pltpu.roll(prev, n, 0), r[:8])
        return jnp.concatenate([top, r[8:]], axis=0)

    for f0 in range(0, D_FF, F_TILE):
        fw = min(F_TILE, D_FF - f0)
        branches = []
        for c0 in (f0, D_FF + f0):
            a = _dot(hb, win_ref[:, c0:c0 + fw])
            prev = carry_ref[:, c0:c0 + fw]
            carry_ref[:, c0:c0 + fw] = a[MLP_TM - 8:, :]
            branches.append(cb_ref[:, c0:c0 + fw]
                            + shifted(a, prev, 2) * cw_ref[0:1, c0:c0 + fw]
                            + shifted(a, prev, 1) * cw_ref[1:2, c0:c0 + fw]
                            + a * cw_ref[2:3, c0:c0 + fw])
        gate, up = branches
        act = (gate * (1.0 / (1.0 + jnp.exp(-gate))) * up).astype(BF16)
        y_ref[...] += _dot(act, wd_ref[f0:f0 + fw, :])


def _mlp(x, o, wo, g, w_in, conv_w, conv_b, w_down, seq):
    m = x.shape[0]
    return pl.pallas_call(
        functools.partial(_mlp_body, blocks_per_seq=seq // MLP_TM),
        grid=(m // MLP_TM,),
        in_specs=[
            pl.BlockSpec((MLP_TM, D_MODEL), lambda i: (i, 0)),
            pl.BlockSpec((MLP_TM, D_MODEL), lambda i: (i, 0)),
            _const_spec((D_MODEL, D_MODEL)),
            _const_spec((1, D_MODEL)),
            _const_spec((D_MODEL, 2 * D_FF)),
            _const_spec((CONV_W, 2 * D_FF)),
            _const_spec((1, 2 * D_FF)),
            _const_spec((D_FF, D_MODEL)),
        ],
        out_specs=pl.BlockSpec((MLP_TM, D_MODEL), lambda i: (i, 0)),
        out_shape=jax.ShapeDtypeStruct((m, D_MODEL), F32),
        scratch_shapes=[pltpu.VMEM((8, 2 * D_FF), F32)],
        compiler_params=_cparams(("arbitrary",)),
        name="oproj_conv_mlp",
    )(x, o, wo, g.reshape(1, D_MODEL), w_in, conv_w, conv_b.reshape(1, 2 * D_FF), w_down)


def _rope_tables(seq):
    inv = ROPE_THETA ** (-jnp.arange(0, HEAD_DIM, 2, dtype=F32) / HEAD_DIM)
    ang = jnp.arange(seq, dtype=F32)[:, None] * inv[None, :]
    cos, sin = jnp.cos(ang), jnp.sin(ang)
    return jnp.tile(cos, (1, 4)), jnp.tile(jnp.concatenate([-sin, sin], axis=1), (1, 2))


def _pair_gain(gq, gk):
    return jnp.stack([jnp.tile(gq.astype(F32), 2), jnp.tile(gk.astype(F32), 2)])


def kernel(x, norm1_g, norm2_g, a_w_qkv, a_q_norm, a_k_norm, a_rel_bias, a_w_o,
           b_w_in, b_q_norm, b_k_norm, b_w_o, c_w_qkv, c_w_o,
           ffn_w_in, ffn_conv_w, ffn_conv_b, ffn_w_down):
    bsz, seq, d_model = x.shape
    assert d_model == D_MODEL and x.dtype == F32
    assert all(seq % blk == 0 for blk in (TM, MLP_TM, A_TQ, B_SCAN, C_TQ)) and PAD % TM == 0
    depth = norm1_g.shape[0]
    rows = bsz * seq
    blk = jnp.arange(LANES) // HEAD_DIM
    bd = (blk[:, None] == blk[None, :]).astype(BF16)
    tri = (jnp.arange(C_TK)[:, None] >= jnp.arange(C_TK)[None, :]).astype(BF16)
    rope_tabs = _rope_tables(seq)
    no_gain = jnp.ones((2, LANES), F32)
    plain = (None, None, 1.0)
    q_scale = HEAD_DIM ** -0.5

    ia = ib = ic = 0
    for layer in range(depth):
        kind = layer % 3
        if kind == 0:
            gains = _pair_gain(a_q_norm[ia], a_k_norm[ia])
            qkv = _project(x, norm1_g[layer], a_w_qkv[ia].astype(BF16), [plain] * (3 * N_PAIRS),
                           no_gain, bd, lead_rows=PAD)
            o = _attn_a(qkv, _band_bias(a_rel_bias[ia]), bd, gains, bsz, seq)
            w_o = a_w_o[ia]
            ia += 1
        elif kind == 1:
            gains = _pair_gain(b_q_norm[ib], b_k_norm[ib])
            n_in = b_w_in.shape[-1]
            n_main = 3 * D_MODEL + IDX_HEADS * IDX_DIM
            w = jnp.pad(b_w_in[ib], ((0, 0), (0, n_main + LANES - n_in))).astype(BF16)
            modes = ([(0, "all", q_scale * LOG2E)] * N_PAIRS + [(1, "all", 1.0)] * N_PAIRS + [plain] * N_PAIRS
                     + [(None, "all", IDX_DIM ** -0.5)] * (IDX_HEADS * IDX_DIM // LANES)
                     + [(None, "low_head", 1.0)])
            q, k, v, qi, ki_tile, tail = _project(
                x, norm1_g[layer], w, modes, gains, bd, rope_tabs=rope_tabs, aux=True,
                splits=(N_PAIRS, N_PAIRS, N_PAIRS, IDX_HEADS * IDX_DIM // LANES, 1))
            nblk = seq // B_TQ
            ki = ki_tile[..., :IDX_DIM]
            wi = tail[..., IDX_DIM:IDX_DIM + IDX_HEADS]
            vt = v.swapaxes(1, 2)
            qi_t = (qi.reshape(bsz, nblk, B_TQ, IDX_HEADS, IDX_DIM).swapaxes(2, 3)
                    .reshape(bsz, nblk, IDX_HEADS * B_TQ, IDX_DIM))
            wi_t = wi.reshape(bsz, nblk, B_TQ, IDX_HEADS).swapaxes(2, 3)
            o = _attn_b(q, qi_t, wi_t, k, vt, ki, bsz, seq, min(TOPK_MAX, seq // 4))
            w_o = b_w_o[ib]
            ib += 1
        else:
            modes = [(None, None, q_scale)] * N_PAIRS + [plain] * (2 * N_PAIRS)
            qkv = _project(x, norm1_g[layer], c_w_qkv[ic].astype(BF16), modes, no_gain, bd)
            o = _attn_c(qkv, tri, bsz, seq)
            w_o = c_w_o[ic]
            ic += 1
        x = _mlp(x.reshape(rows, D_MODEL), o.reshape(rows, D_MODEL), w_o.astype(BF16), norm2_g[layer],
                 ffn_w_in[layer].astype(BF16), ffn_conv_w[layer], ffn_conv_b[layer],
                 ffn_w_down[layer].astype(BF16), seq).reshape(bsz, seq, D_MODEL)
    return x
```

```python
import functools

import jax
import jax.numpy as jnp
from jax import lax
from jax.experimental import pallas as pl
from jax.experimental.pallas import tpu as pltpu

D_MODEL = 1024
N_HEADS = 16
HEAD_DIM = 64
N_PAIRS = N_HEADS // 2
LANES = 128
CHUNK = 64
LEFT_CHUNKS = 8
PAD = LEFT_CHUNKS * CHUNK
MAX_REL = 256
IDX_HEADS = 8
IDX_DIM = 64
TOPK_MAX = 256
D_FF = 2816
CONV_W = 3
ROPE_THETA = 10000.0
EPS = 1e-6
NEG = -1e30

TM = 512
A_TQ = 128
A_W = A_TQ + PAD
A_GROUP = 8
C_GROUP = 4
LOG2E = 1.4426950408889634
B_TQ = 128
B_TK = 128
B_SCAN = 512
B_UNROLL = 4
MLP_TM = 1024
C_TQ = 256
C_TK = 256
F_TILE = 1408
VMEM_LIMIT = 56 * 1024 * 1024

BF16 = jnp.bfloat16
F32 = jnp.float32


def _cparams(sem):
    return pltpu.CompilerParams(dimension_semantics=sem, vmem_limit_bytes=VMEM_LIMIT)


def _const_spec(shape):
    nd = len(shape)
    return pl.BlockSpec(shape, lambda *_: (0,) * nd, pipeline_mode=pl.Buffered(1))


def _dot(a, b):
    return jnp.dot(a, b, preferred_element_type=F32)


def _dot_nt(a, b):
    return lax.dot_general(a, b, (((1,), (1,)), ((), ())), preferred_element_type=F32)


def _rms(x, g):
    return x * lax.rsqrt(jnp.mean(x * x, axis=-1, keepdims=True) + EPS) * g


def _lane_half_masks(dtype):
    lane = lax.broadcasted_iota(jnp.int32, (1, LANES), 1)
    lo = (lane < HEAD_DIM).astype(dtype)
    return lo, (1 - lo).astype(dtype)


def _head_norm(y, bd, gain):
    ss = _dot((y * y).astype(BF16), bd)
    return y * lax.rsqrt(ss * (1.0 / HEAD_DIM) + EPS) * gain


def _rope(y, cos_t, sin_t):
    lane = lax.broadcasted_iota(jnp.int32, y.shape, 1)
    first = (lane % HEAD_DIM) < (HEAD_DIM // 2)
    partner = jnp.where(first, pltpu.roll(y, LANES - HEAD_DIM // 2, 1), pltpu.roll(y, HEAD_DIM // 2, 1))
    return y * cos_t + partner * sin_t


def _proj_body(*refs, modes, splits, use_rope, n_aux, lead):
    it = iter(refs)
    x_ref, g_ref, w_ref, bd_ref, gain_ref = next(it), next(it), next(it), next(it), next(it)
    cos_ref = sin_ref = None
    if use_rope:
        cos_ref, sin_ref = next(it), next(it)
    o_refs = [next(it) for _ in splits]
    aux_ref = next(it) if n_aux else None
    n_out = 128 * len(modes)
    dest = [(k, t) for k, n_tiles in enumerate(splits) for t in range(n_tiles)]

    def compute():
        hb = _rms(x_ref[...], g_ref[...]).astype(BF16)
        bd = bd_ref[...]
        for c0 in range(0, n_out, 512):
            wd = min(512, n_out - c0)
            y = _dot(hb, w_ref[:, c0:c0 + wd])
            for t0 in range(0, wd, LANES):
                norm_idx, rope, scale = modes[(c0 + t0) // LANES]
                yt = y[:, t0:t0 + LANES]
                if norm_idx is not None:
                    yt = _head_norm(yt, bd, gain_ref[norm_idx:norm_idx + 1, :])
                if rope == "all":
                    yt = _rope(yt, cos_ref[...], sin_ref[...])
                elif rope == "low_head":
                    lane = lax.broadcasted_iota(jnp.int32, yt.shape, 1)
                    yt = jnp.where(lane < HEAD_DIM, _rope(yt, cos_ref[...], sin_ref[...]), yt)
                if scale != 1.0:
                    yt = yt * scale
                k, t = dest[(c0 + t0) // LANES]
                o_refs[k][:, t * LANES:(t + 1) * LANES] = yt.astype(o_refs[k].dtype)
                if n_aux and (c0 + t0) // LANES == len(modes) - 1:
                    aux_ref[...] = yt

    if lead:
        @pl.when(pl.program_id(1) < lead)
        def _():
            for o_ref in o_refs:
                o_ref[...] = jnp.zeros_like(o_ref)

        pl.when(pl.program_id(1) >= lead)(compute)
    else:
        compute()


def _project(x, g, w, modes, gains, bd, rope_tabs=None, aux=False, lead_rows=0, splits=None):
    bsz, seq, _ = x.shape
    n_out = LANES * len(modes)
    splits = tuple(splits) if splits is not None else (len(modes),)
    assert sum(splits) == len(modes)
    lead = lead_rows // TM
    nblk = seq // TM
    in_specs = [
        pl.BlockSpec((None, TM, D_MODEL), lambda b, i: (b, jnp.maximum(i - lead, 0), 0)),
        _const_spec((1, D_MODEL)),
        _const_spec((D_MODEL, n_out)),
        _const_spec((LANES, LANES)),
        _const_spec(gains.shape),
    ]
    args = [x, g.reshape(1, D_MODEL), w, bd, gains]
    if rope_tabs is not None:
        in_specs += [pl.BlockSpec((TM, LANES), lambda b, i: (i, 0))] * 2
        args += list(rope_tabs)
    out_shape = [jax.ShapeDtypeStruct((bsz, lead_rows + seq, n * LANES), BF16) for n in splits]
    out_specs = [pl.BlockSpec((None, TM, n * LANES), lambda b, i: (b, i, 0)) for n in splits]
    if aux:
        out_shape.append(jax.ShapeDtypeStruct((bsz, seq, LANES), F32))
        out_specs.append(pl.BlockSpec((None, TM, LANES), lambda b, i: (b, i, 0)))
    res = pl.pallas_call(
        functools.partial(_proj_body, modes=tuple(modes), splits=splits, use_rope=rope_tabs is not None,
                          n_aux=int(aux), lead=lead),
        grid=(bsz, lead + nblk),
        in_specs=in_specs,
        out_specs=out_specs,
        out_shape=out_shape,
        compiler_params=_cparams(("parallel", "arbitrary")),
        name="norm_proj",
    )(*args)
    return res if (aux or len(splits) > 1) else res[0]


def _attn_a_body(q_ref, k_ref, v_ref, bias_ref, bd_ref, gain_ref, o_ref, kn_ref):
    i = pl.program_id(2)
    bd = bd_ref[...]
    n_rows = kn_ref.shape[0]

    @pl.when(i == 0)
    def _():
        def norm_rows(c, carry):
            r0 = pl.multiple_of(c * TM, TM)
            for g in range(A_GROUP):
                kk = k_ref[pl.ds(r0, TM), g * LANES:(g + 1) * LANES].astype(F32)
                kn_ref[pl.ds(r0, TM), g * LANES:(g + 1) * LANES] = (
                    _head_norm(kk, bd, gain_ref[1:2, :]).astype(BF16))
            return carry
        lax.fori_loop(0, n_rows // TM, norm_rows, 0)

    lo, hi = _lane_half_masks(F32)
    w0 = pl.multiple_of(i * A_TQ, A_TQ)
    first_head = lax.broadcasted_iota(jnp.int32, (A_TQ, LANES), 1) < HEAD_DIM
    ones = jnp.ones((A_W, LANES), BF16)

    col = lax.broadcasted_iota(jnp.int32, (1, A_W), 1)
    valid = (col + (i * A_TQ - PAD)) >= 0
    outs = []
    for g in range(A_GROUP):
        lanes = slice(g * LANES, (g + 1) * LANES)
        q = _head_norm(q_ref[:, lanes].astype(F32), bd, gain_ref[0:1, :]) * (LOG2E * HEAD_DIM ** -0.5)
        q2 = jnp.concatenate([q * lo, q * hi], axis=0).astype(BF16)
        s = _dot_nt(q2, kn_ref[pl.ds(w0, A_W), lanes]) + bias_ref[g]
        s = jnp.where(valid, s, NEG)
        p = jnp.exp2(s - jnp.max(s, axis=-1, keepdims=True)).astype(BF16)
        v_aug = jnp.concatenate([v_ref[pl.ds(w0, A_W), lanes], ones], axis=1)
        o2 = _dot(p, v_aug)
        o2 = o2[:, :LANES] * (1.0 / o2[:, LANES:])
        outs.append(jnp.where(first_head, o2[:A_TQ], o2[A_TQ:]).astype(o_ref.dtype))
    o_ref[...] = jnp.concatenate(outs, axis=1)


def _attn_a(qkv, bias, bd, gains, bsz, seq):
    rows = PAD + seq
    width = A_GROUP * LANES
    groups = N_PAIRS // A_GROUP
    return pl.pallas_call(
        _attn_a_body,
        grid=(bsz, groups, seq // A_TQ),
        in_specs=[
            pl.BlockSpec((None, A_TQ, width), lambda b, j, i: (b, i + PAD // A_TQ, j)),
            pl.BlockSpec((None, rows, width), lambda b, j, i: (b, 0, groups + j), pipeline_mode=pl.Buffered(1)),
            pl.BlockSpec((None, rows, width), lambda b, j, i: (b, 0, 2 * groups + j), pipeline_mode=pl.Buffered(1)),
            pl.BlockSpec((A_GROUP, 2 * A_TQ, A_W), lambda b, j, i: (j, 0, 0), pipeline_mode=pl.Buffered(1)),
            _const_spec((LANES, LANES)),
            _const_spec(gains.shape),
        ],
        out_specs=pl.BlockSpec((None, A_TQ, width), lambda b, j, i: (b, i, j)),
        out_shape=jax.ShapeDtypeStruct((bsz, seq, D_MODEL), BF16),
        scratch_shapes=[pltpu.VMEM((rows, width), BF16)],
        compiler_params=_cparams(("parallel", "parallel", "arbitrary")),
        name="attn_band",
    )(qkv, qkv, qkv, bias, bd, gains)


def _band_bias(rel_bias):
    r = jnp.arange(A_TQ)[:, None]
    c = jnp.arange(A_W)[None, :]
    in_band = (c // CHUNK >= r // CHUNK) & (c // CHUNK <= r // CHUNK + LEFT_CHUNKS)
    period = A_W + A_TQ
    m = jnp.arange(period)
    c_minus_r = jnp.where(m < A_W, m, m - period)
    idx = jnp.clip(PAD - c_minus_r, -MAX_REL, MAX_REL) + MAX_REL
    line = rel_bias.astype(F32)[:, idx] * LOG2E
    toep = jnp.tile(line, (1, A_TQ))[:, :A_TQ * (period - 1)].reshape(-1, A_TQ, period - 1)[:, :, :A_W]
    bias = jnp.where(in_band[None], toep, NEG)
    return bias.reshape(N_PAIRS, 2 * A_TQ, A_W)


def _softplus(z):
    return jnp.maximum(z, 0.0) + jnp.log(1.0 + jnp.exp2(jnp.abs(z) * (-LOG2E)))


def _attn_c_body(q_ref, k_ref, v_ref, tri_ref, o_ref, acc_ref):
    i = pl.program_id(2)
    tri = tri_ref[...]
    lo, hi = _lane_half_masks(BF16)
    q = q_ref[...]
    q_pairs = [jnp.concatenate([q[:, g * LANES:(g + 1) * LANES] * lo,
                                q[:, g * LANES:(g + 1) * LANES] * hi], axis=0) for g in range(C_GROUP)]
    row = lax.broadcasted_iota(jnp.int32, (2 * C_TQ, C_TK), 0)
    colk = lax.broadcasted_iota(jnp.int32, (2 * C_TQ, C_TK), 1)
    causal = colk < jnp.where(row >= C_TQ, row - C_TQ, row)
    first_head = lax.broadcasted_iota(jnp.int32, (C_TQ, LANES), 1) < HEAD_DIM

    def block(kb, runs, diag):
        r0 = pl.multiple_of(kb * C_TK, C_TK)
        pv = []
        new_runs = []
        for g, (q2, run) in enumerate(zip(q_pairs, runs)):
            lanes = slice(g * LANES, (g + 1) * LANES)
            z = _dot_nt(q2, k_ref[pl.ds(r0, C_TK), lanes])
            sp = _softplus(z)
            if diag:
                sp = jnp.where(causal, sp, 0.0)
            tail = _dot(sp.astype(BF16), tri)
            a = jnp.exp(z - (tail + run))
            if diag:
                a = jnp.where(causal, a, 0.0)
            c2 = _dot(a.astype(BF16), v_ref[pl.ds(r0, C_TK), lanes])
            pv.append(jnp.where(first_head, c2[:C_TQ], c2[C_TQ:]))
            new_runs.append(run + jnp.sum(sp, axis=-1, keepdims=True))
        return pv, tuple(new_runs)

    zero_run = jnp.zeros((2 * C_TQ, 1), F32)
    pv, runs = block(i, (zero_run,) * C_GROUP, True)
    acc_ref[...] = jnp.concatenate(pv, axis=1)

    def single(runs):
        pv, runs = block(i - 1, runs, False)
        acc_ref[...] += jnp.concatenate(pv, axis=1)
        return runs

    odd = i % 2
    runs = lax.cond(odd == 1, single, lambda r: r, runs)

    def step(n, runs):
        kb = i - 1 - odd - 2 * n
        pv_a, runs = block(kb, runs, False)
        pv_b, runs = block(kb - 1, runs, False)
        acc_ref[...] += jnp.concatenate(pv_a, axis=1) + jnp.concatenate(pv_b, axis=1)
        return runs

    lax.fori_loop(0, i // 2, step, runs)
    o_ref[...] = acc_ref[...].astype(o_ref.dtype)


def _attn_c(qkv, tri, bsz, seq):
    width = C_GROUP * LANES
    groups = N_PAIRS // C_GROUP
    return pl.pallas_call(
        _attn_c_body,
        grid=(bsz, groups, seq // C_TQ),
        in_specs=[
            pl.BlockSpec((None, C_TQ, width), lambda b, j, i: (b, i, j)),
            pl.BlockSpec((None, seq, width), lambda b, j, i: (b, 0, groups + j)),
            pl.BlockSpec((None, seq, width), lambda b, j, i: (b, 0, 2 * groups + j)),
            _const_spec((C_TK, C_TK)),
        ],
        out_specs=pl.BlockSpec((None, C_TQ, width), lambda b, j, i: (b, i, j)),
        out_shape=jax.ShapeDtypeStruct((bsz, seq, D_MODEL), BF16),
        scratch_shapes=[pltpu.VMEM((C_TQ, width), F32)],
        compiler_params=_cparams(("parallel", "parallel", "arbitrary")),
        name="attn_stick",
    )(qkv, qkv, qkv, tri)


def _attn_b_body(q_ref, qi_ref, wi_ref, k_ref, vt_ref, ki_ref, o_ref,
                 key_ref, hi16_ref, lo16_ref, bias_ref, qm_ref, m_ref, l_ref, acc_ref, *, topk):
    i = pl.program_id(1)
    nkb = i + 1
    lane_q = lax.broadcasted_iota(jnp.int32, (1, B_TQ), 1)
    limit = ((i * B_TQ + lane_q) // CHUNK + 1) * CHUNK
    row_k = lax.broadcasted_iota(jnp.int32, (B_TK, B_TQ), 0)
    qi = qi_ref[...]
    wi = wi_ref[...] * (IDX_HEADS ** -0.5)

    def score_group(sb, carry):
        for u in range(B_SCAN // B_TK):
            r0 = pl.multiple_of(sb * B_SCAN + u * B_TK, B_TK)
            logits = _dot_nt(ki_ref[pl.ds(r0, B_TK), :], qi)
            score = jnp.zeros((B_TK, B_TQ), F32)
            for h in range(IDX_HEADS):
                score = score + wi[h:h + 1, :] * jnp.maximum(logits[:, h * B_TQ:(h + 1) * B_TQ], 0.0)
            bits = pltpu.bitcast(score, jnp.int32)
            okey = bits ^ ((bits >> 31) & jnp.int32(0x7FFFFFFF))
            adm = (row_k + r0) < limit
            key_ref[pl.ds(r0, B_TK), :] = jnp.where(adm, okey, jnp.int32(-(2 ** 31)))
        return carry

    n_scan = (i + B_SCAN // B_TK) // (B_SCAN // B_TK)
    lax.fori_loop(0, n_scan, score_group, 0)
    row_s = lax.broadcasted_iota(jnp.int32, (B_SCAN, B_TQ), 0)

    def count(pred_fn):
        def body(sb, c):
            r0 = pl.multiple_of(sb * B_SCAN, B_SCAN)
            hit = pred_fn(key_ref[pl.ds(r0, B_SCAN), :], row_s + sb * B_SCAN)
            return c + jnp.sum(jnp.where(hit, 1, 0).reshape(B_SCAN // 8, 8, B_TQ), axis=0)
        part = lax.fori_loop(0, n_scan, body, jnp.zeros((8, B_TQ), jnp.int32))
        return jnp.sum(part, axis=0, keepdims=True)

    i16 = jnp.int16
    half_min = -(2 ** 15)

    def split_group(sb, carry):
        r0 = pl.multiple_of(sb * B_SCAN, B_SCAN)
        kk = key_ref[pl.ds(r0, B_SCAN), :]
        hi16_ref[pl.ds(r0, B_SCAN), :] = (kk >> 16).astype(i16)
        lo16_ref[pl.ds(r0, B_SCAN), :] = ((kk & 0xFFFF) + half_min).astype(i16)
        return carry

    lax.fori_loop(0, n_scan, split_group, 0)

    def count16(ref, pred_fn):
        def body(sb, c):
            r0 = pl.multiple_of(sb * B_SCAN, B_SCAN)
            hit = pred_fn(ref[pl.ds(r0, B_SCAN), :])
            ones = jnp.where(hit, jnp.ones((B_SCAN, B_TQ), i16), jnp.zeros((B_SCAN, B_TQ), i16))
            while ones.shape[0] > 16:
                half = ones.shape[0] // 2
                ones = ones[:half] + ones[half:]
            return c + ones
        part = lax.fori_loop(0, n_scan, body, jnp.zeros((16, B_TQ), i16))
        return jnp.sum(part.astype(jnp.int32), axis=0, keepdims=True)

    def search16(ref, rank):
        def bit_step(n, t):
            cand = t | (jnp.int32(1) << (15 - n))
            cand16 = (cand + half_min).astype(i16)
            cnt = count16(ref, lambda v: v >= cand16)
            return jnp.where(cnt >= rank, cand, t)
        return lax.fori_loop(0, 16, bit_step, jnp.zeros((1, B_TQ), jnp.int32)) + half_min

    thr_hi = search16(hi16_ref, topk)
    thr_hi16 = thr_hi.astype(i16)
    rank_lo = topk - count16(hi16_ref, lambda v: v > thr_hi16)

    def mask_group(sb, carry):
        r0 = pl.multiple_of(sb * B_SCAN, B_SCAN)
        keep = hi16_ref[pl.ds(r0, B_SCAN), :] == thr_hi16
        lo16_ref[pl.ds(r0, B_SCAN), :] = jnp.where(keep, lo16_ref[pl.ds(r0, B_SCAN), :],
                                                   jnp.full((B_SCAN, B_TQ), half_min, i16))
        return carry

    lax.fori_loop(0, n_scan, mask_group, 0)
    thr_lo = search16(lo16_ref, rank_lo)
    thr = (thr_hi << 16) | (thr_lo - half_min)
    n_gt = count(lambda kk, pos: kk > thr)
    n_ge = count(lambda kk, pos: kk >= thr)
    need = topk - n_gt

    def tie_search():
        def idx_step(n, j):
            cand = j | (jnp.int32(1) << (15 - n))
            cnt = count(lambda kk, pos: (kk == thr) & (pos < cand))
            return jnp.where(cnt < need, cand, j)
        return lax.fori_loop(0, 16, idx_step, jnp.zeros((1, B_TQ), jnp.int32))

    tie_hi = lax.cond(jnp.max(n_ge) > topk, tie_search,
                      lambda: jnp.full((1, B_TQ), 2 ** 30, jnp.int32))

    def bias_block(kb, carry):
        r0 = pl.multiple_of(kb * B_TK, B_TK)
        kk = key_ref[pl.ds(r0, B_TK), :]
        pos = row_k + kb * B_TK
        sel = (kk > thr) | ((kk == thr) & (pos <= tie_hi))
        sel = sel & (pos < limit)
        bias_ref[pl.ds(r0, B_TK), :] = jnp.where(sel, 0.0, NEG)
        return carry

    n_att = (nkb + B_UNROLL - 1) // B_UNROLL
    lax.fori_loop(0, n_att * B_UNROLL, bias_block, 0)

    sub = lax.broadcasted_iota(jnp.int32, (LANES, 1), 0)
    for j in range(N_PAIRS):
        qt = q_ref[:, j * LANES:(j + 1) * LANES].astype(F32).T.astype(BF16)
        for half in range(2):
            qm_ref[2 * j + half] = jnp.where((sub < HEAD_DIM) == (half == 0), qt, jnp.zeros_like(qt))
    m_ref[...] = jnp.full(m_ref.shape, NEG, F32)
    l_ref[...] = jnp.zeros(l_ref.shape, F32)
    acc_ref[...] = jnp.zeros(acc_ref.shape, F32)

    def kv_step(n, carry):
        starts = [pl.multiple_of((n * B_UNROLL + u) * B_TK, B_TK) for u in range(B_UNROLL)]
        biases = [bias_ref[pl.ds(r0, B_TK), :] for r0 in starts]
        for j in range(N_PAIRS):
            k_blks = [k_ref[pl.ds(r0, B_TK), j * LANES:(j + 1) * LANES] for r0 in starts]
            for half in range(2):
                h = 2 * j + half
                d0 = h * HEAD_DIM
                ss = [_dot(k_blk, qm_ref[h]) + bias for k_blk, bias in zip(k_blks, biases)]
                m_old = m_ref[h:h + 1, :]
                m_new = m_old
                for s in ss:
                    m_new = jnp.maximum(m_new, jnp.max(s, axis=0, keepdims=True))
                alpha = jnp.exp2(m_old - m_new)
                ps = [jnp.exp2(s - m_new) for s in ss]
                l_new = alpha * l_ref[h:h + 1, :]
                pv = alpha * acc_ref[d0:d0 + HEAD_DIM, :]
                for r0, p in zip(starts, ps):
                    l_new = l_new + jnp.sum(p, axis=0, keepdims=True)
                    pv = pv + _dot(vt_ref[d0:d0 + HEAD_DIM, pl.ds(r0, B_TK)], p.astype(BF16))
                l_ref[h:h + 1, :] = l_new
                m_ref[h:h + 1, :] = m_new
                acc_ref[d0:d0 + HEAD_DIM, :] = pv
        return carry

    lax.fori_loop(0, n_att, kv_step, 0)
    for j in range(N_PAIRS):
        d0 = j * LANES
        pair = jnp.concatenate(
            [acc_ref[d0 + u * HEAD_DIM:d0 + (u + 1) * HEAD_DIM, :] * (1.0 / l_ref[2 * j + u:2 * j + u + 1, :])
             for u in range(2)], axis=0)
        o_ref[:, d0:d0 + LANES] = pair.T.astype(o_ref.dtype)


def _attn_b(q, qi, wi, k, vt, ki, bsz, seq, topk):
    nblk = seq // B_TQ
    one = pl.Buffered(1)
    return pl.pallas_call(
        functools.partial(_attn_b_body, topk=topk),
        grid=(bsz, nblk),
        in_specs=[
            pl.BlockSpec((None, B_TQ, D_MODEL), lambda b, i: (b, i, 0)),
            pl.BlockSpec((None, None, IDX_HEADS * B_TQ, IDX_DIM), lambda b, i: (b, i, 0, 0)),
            pl.BlockSpec((None, None, IDX_HEADS, B_TQ), lambda b, i: (b, i, 0, 0)),
            pl.BlockSpec((None, seq, D_MODEL), lambda b, i: (b, 0, 0), pipeline_mode=one),
            pl.BlockSpec((None, D_MODEL, seq), lambda b, i: (b, 0, 0), pipeline_mode=one),
            pl.BlockSpec((None, seq, IDX_DIM), lambda b, i: (b, 0, 0), pipeline_mode=one),
        ],
        out_specs=pl.BlockSpec((None, B_TQ, D_MODEL), lambda b, i: (b, i, 0)),
        out_shape=jax.ShapeDtypeStruct((bsz, seq, D_MODEL), BF16),
        scratch_shapes=[
            pltpu.VMEM((seq, B_TQ), jnp.int32),
            pltpu.VMEM((seq, B_TQ), jnp.int16),
            pltpu.VMEM((seq, B_TQ), jnp.int16),
            pltpu.VMEM((seq, B_TQ), F32),
            pltpu.VMEM((N_HEADS, LANES, B_TQ), BF16),
            pltpu.VMEM((N_HEADS, B_TQ), F32),
            pltpu.VMEM((N_HEADS, B_TQ), F32),
            pltpu.VMEM((D_MODEL, B_TQ), F32),
        ],
        compiler_params=_cparams(("parallel", "arbitrary")),
        name="attn_topk",
    )(q, qi, wi, k, vt, ki)


def _mlp_body(x_ref, o_ref, wo_ref, g_ref, win_ref, cw_ref, cb_ref, wd_ref, y_ref, carry_ref, *, blocks_per_seq):
    i = pl.program_id(0)

    @pl.when((i % blocks_per_seq) == 0)
    def _():
        carry_ref[...] = jnp.zeros_like(carry_ref)

    x1 = x_ref[...] + _dot(o_ref[...], wo_ref[...])
    y_ref[...] = x1
    hb = _rms(x1, g_ref[...]).astype(BF16)
    def shifted(a, prev, n):
        row = lax.broadcasted_iota(jnp.int32, prev.shape, 0)
        r = pltpu.roll(a, n, 0)
        top = jnp.where(row < n, pltpu.roll(prev, n, 0), r[:8])
        return jnp.concatenate([top, r[8:]], axis=0)

    for f0 in range(0, D_FF, F_TILE):
        fw = min(F_TILE, D_FF - f0)
        branches = []
        for c0 in (f0, D_FF + f0):
            a = _dot(hb, win_ref[:, c0:c0 + fw])
            prev = carry_ref[:, c0:c0 + fw]
            carry_ref[:, c0:c0 + fw] = a[MLP_TM - 8:, :]
            branches.append(cb_ref[:, c0:c0 + fw]
                            + shifted(a, prev, 2) * cw_ref[0:1, c0:c0 + fw]
                            + shifted(a, prev, 1) * cw_ref[1:2, c0:c0 + fw]
                            + a * cw_ref[2:3, c0:c0 + fw])
        gate, up = branches
        act = (gate * (1.0 / (1.0 + jnp.exp(-gate))) * up).astype(BF16)
        y_ref[...] += _dot(act, wd_ref[f0:f0 + fw, :])


def _mlp(x, o, wo, g, w_in, conv_w, conv_b, w_down, seq):
    m = x.shape[0]
    return pl.pallas_call(
        functools.partial(_mlp_body, blocks_per_seq=seq // MLP_TM),
        grid=(m // MLP_TM,),
        in_specs=[
            pl.BlockSpec((MLP_TM, D_MODEL), lambda i: (i, 0)),
            pl.BlockSpec((MLP_TM, D_MODEL), lambda i: (i, 0)),
            _const_spec((D_MODEL, D_MODEL)),
            _const_spec((1, D_MODEL)),
            _const_spec((D_MODEL, 2 * D_FF)),
            _const_spec((CONV_W, 2 * D_FF)),
            _const_spec((1, 2 * D_FF)),
            _const_spec((D_FF, D_MODEL)),
        ],
        out_specs=pl.BlockSpec((MLP_TM, D_MODEL), lambda i: (i, 0)),
        out_shape=jax.ShapeDtypeStruct((m, D_MODEL), F32),
        scratch_shapes=[pltpu.VMEM((8, 2 * D_FF), F32)],
        compiler_params=_cparams(("arbitrary",)),
        name="oproj_conv_mlp",
    )(x, o, wo, g.reshape(1, D_MODEL), w_in, conv_w, conv_b.reshape(1, 2 * D_FF), w_down)


def _rope_tables(seq):
    inv = ROPE_THETA ** (-jnp.arange(0, HEAD_DIM, 2, dtype=F32) / HEAD_DIM)
    ang = jnp.arange(seq, dtype=F32)[:, None] * inv[None, :]
    cos, sin = jnp.cos(ang), jnp.sin(ang)
    return jnp.tile(cos, (1, 4)), jnp.tile(jnp.concatenate([-sin, sin], axis=1), (1, 2))


def _pair_gain(gq, gk):
    return jnp.stack([jnp.tile(gq.astype(F32), 2), jnp.tile(gk.astype(F32), 2)])


def kernel(x, norm1_g, norm2_g, a_w_qkv, a_q_norm, a_k_norm, a_rel_bias, a_w_o,
           b_w_in, b_q_norm, b_k_norm, b_w_o, c_w_qkv, c_w_o,
           ffn_w_in, ffn_conv_w, ffn_conv_b, ffn_w_down):
    bsz, seq, d_model = x.shape
    assert d_model == D_MODEL and x.dtype == F32
    assert all(seq % blk == 0 for blk in (TM, MLP_TM, A_TQ, B_SCAN, C_TQ)) and PAD % TM == 0
    assert seq < 2 ** 15
    depth = norm1_g.shape[0]
    rows = bsz * seq
    blk = jnp.arange(LANES) // HEAD_DIM
    bd = (blk[:, None] == blk[None, :]).astype(BF16)
    tri = (jnp.arange(C_TK)[:, None] >= jnp.arange(C_TK)[None, :]).astype(BF16)
    rope_tabs = _rope_tables(seq)
    no_gain = jnp.ones((2, LANES), F32)
    plain = (None, None, 1.0)
    q_scale = HEAD_DIM ** -0.5

    ia = ib = ic = 0
    for layer in range(depth):
        kind = layer % 3
        if kind == 0:
            gains = _pair_gain(a_q_norm[ia], a_k_norm[ia])
            qkv = _project(x, norm1_g[layer], a_w_qkv[ia].astype(BF16), [plain] * (3 * N_PAIRS),
                           no_gain, bd, lead_rows=PAD)
            o = _attn_a(qkv, _band_bias(a_rel_bias[ia]), bd, gains, bsz, seq)
            w_o = a_w_o[ia]
            ia += 1
        elif kind == 1:
            gains = _pair_gain(b_q_norm[ib], b_k_norm[ib])
            n_in = b_w_in.shape[-1]
            n_main = 3 * D_MODEL + IDX_HEADS * IDX_DIM
            w = jnp.pad(b_w_in[ib], ((0, 0), (0, n_main + LANES - n_in))).astype(BF16)
            modes = ([(0, "all", q_scale * LOG2E)] * N_PAIRS + [(1, "all", 1.0)] * N_PAIRS + [plain] * N_PAIRS
                     + [(None, "all", IDX_DIM ** -0.5)] * (IDX_HEADS * IDX_DIM // LANES)
                     + [(None, "low_head", 1.0)])
            q, k, v, qi, ki_tile, tail = _project(
                x, norm1_g[layer], w, modes, gains, bd, rope_tabs=rope_tabs, aux=True,
                splits=(N_PAIRS, N_PAIRS, N_PAIRS, IDX_HEADS * IDX_DIM // LANES, 1))
            nblk = seq // B_TQ
            ki = ki_tile[..., :IDX_DIM]
            wi = tail[..., IDX_DIM:IDX_DIM + IDX_HEADS]
            vt = v.swapaxes(1, 2)
            qi_t = (qi.reshape(bsz, nblk, B_TQ, IDX_HEADS, IDX_DIM).swapaxes(2, 3)
                    .reshape(bsz, nblk, IDX_HEADS * B_TQ, IDX_DIM))
            wi_t = wi.reshape(bsz, nblk, B_TQ, IDX_HEADS).swapaxes(2, 3)
            o = _attn_b(q, qi_t, wi_t, k, vt, ki, bsz, seq, min(TOPK_MAX, seq // 4))
            w_o = b_w_o[ib]
            ib += 1
        else:
            modes = [(None, None, q_scale)] * N_PAIRS + [plain] * (2 * N_PAIRS)
            qkv = _project(x, norm1_g[layer], c_w_qkv[ic].astype(BF16), modes, no_gain, bd)
            o = _attn_c(qkv, tri, bsz, seq)
            w_o = c_w_o[ic]
            ic += 1
        x = _mlp(x.reshape(rows, D_MODEL), o.reshape(rows, D_MODEL), w_o.astype(BF16), norm2_g[layer],
                 ffn_w_in[layer].astype(BF16), ffn_conv_w[layer], ffn_conv_b[layer],
                 ffn_w_down[layer].astype(BF16), seq).reshape(bsz, seq, D_MODEL)
    return x
```
